```python
import math
import jax
import jax.numpy as jnp
from jax import lax
import numpy as np

D_MODEL = 2048
BATCH = 2
SEQ = 4096
DEPTH = 4
DEC_BATCH = 32
DEC_SEQ = 64
PAST_LEN = 2048

CHUNK = 64
Q_BLOCK = 128
HEAD_DIM = 128
BRANCH_WIDTH = D_MODEL // 2
N_BRANCH = 3
HG_HEADS = BRANCH_WIDTH // HEAD_DIM
HG_BLOCK = 16
DIFF_HEADS = BRANCH_WIDTH // (2 * HEAD_DIM)
FOX_HEADS = BRANCH_WIDTH // HEAD_DIM
D_FF = ((8 * D_MODEL // 3 + 127) // 128) * 128
NUM_BUCKETS = 32
MAX_DISTANCE = 128
N_ADA = 9
EPS = 1e-6
IN_SIZES = (BRANCH_WIDTH,) * 10 + (FOX_HEADS, N_BRANCH * D_MODEL)
IN_SPLITS = tuple(int(s) for s in np.cumsum(IN_SIZES)[:-1])
N_IN = int(sum(IN_SIZES))

kernel_name = 'hybrid_streaming_encoder_step'


def rms_norm(x, g):
    xf = x.astype(jnp.float32)
    y = xf * lax.rsqrt(jnp.mean(xf * xf, axis=-1, keepdims=True) + EPS)
    return (y * g.astype(jnp.float32)).astype(x.dtype)


def swiglu(h, w_up, w_down):
    a, b = jnp.split(h @ w_up, 2, axis=-1)
    return (jax.nn.silu(a) * b) @ w_down


def t5_bucket(rel):
    half = NUM_BUCKETS // 2
    max_exact = half // 2
    ret = jnp.where(rel > 0, half, 0)
    n = jnp.abs(rel)
    nf = jnp.maximum(n, 1).astype(jnp.float32)
    large = max_exact + (jnp.log(nf / max_exact) / math.log(MAX_DISTANCE / max_exact) * (half - max_exact)).astype(jnp.int32)
    large = jnp.minimum(large, half - 1)
    return ret + jnp.where(n < max_exact, n, large)


def hgrn_lower_bounds(lb_logits):
    cs = jnp.cumsum(jax.nn.softmax(lb_logits.astype(jnp.float32), axis=0), axis=0)
    return cs - cs[0:1]


def sweep_queries(fn, q_args, q_pos):
    tq = q_pos.shape[0]
    if tq <= Q_BLOCK or tq % Q_BLOCK:
        return fn(*q_args, q_pos)
    nb = tq // Q_BLOCK

    def to_blocks(a):
        return jnp.moveaxis(a.reshape(a.shape[0], nb, Q_BLOCK, *a.shape[2:]), 1, 0)

    out = lax.map(lambda xs: fn(*xs[0], xs[1]), (tuple(to_blocks(a) for a in q_args), q_pos.reshape(nb, Q_BLOCK)))
    out = jnp.moveaxis(out, 0, 1)
    return out.reshape(out.shape[0], tq, *out.shape[3:])


def diff_core(q, q_pos, k, v, k_pos, bias_table, lam, lam_init, g_o):
    logits = jnp.einsum('bqhcd,bkhcd->bhcqk', q, k).astype(jnp.float32) * (HEAD_DIM ** -0.5)
    bias = jnp.transpose(bias_table[t5_bucket(k_pos[None, :] - q_pos[:, None])], (2, 0, 1)).astype(jnp.float32)
    mask = (k_pos[None, :] // CHUNK) <= (q_pos[:, None] // CHUNK)
    logits = jnp.where(mask, logits + bias[None, :, None], -jnp.inf)
    p = jax.nn.softmax(logits, axis=-1)
    a = p[:, :, 0] - lam * p[:, :, 1]
    o = jnp.einsum('bhqk,bkhe->bqhe', a.astype(v.dtype), v)
    return rms_norm(o, g_o) * (1.0 - lam_init)


def fox_core(q, cq, q_pos, k, v, ck, k_pos):
    logits = jnp.einsum('bqhd,bkhd->bhqk', q, k).astype(jnp.float32) * (HEAD_DIM ** -0.5)
    decay = jnp.swapaxes(cq, 1, 2)[..., :, None] - jnp.swapaxes(ck, 1, 2)[..., None, :]
    mask = k_pos[None, :] <= q_pos[:, None]
    p = jax.nn.softmax(jnp.where(mask, logits + decay, -jnp.inf), axis=-1)
    return jnp.einsum('bhqk,bkhd->bqhd', p.astype(v.dtype), v)


def hgrn2_scan(q, k, v, g, s0):
    B, T = q.shape[0], q.shape[1]
    pad = (-T) % HG_BLOCK
    if pad:
        padw = ((0, 0), (0, pad), (0, 0), (0, 0))
        q, k, v, g = [jnp.pad(a, padw) for a in (q, k, v, g)]
    n = (T + pad) // HG_BLOCK

    def to_blocks(a):
        return jnp.transpose(a.reshape(B, n, HG_BLOCK, *a.shape[2:]), (1, 0, 3, 2, 4))

    causal = jnp.tril(jnp.ones((HG_BLOCK, HG_BLOCK), dtype=bool))

    def step(S, blk):
        qb, kb, vb, gb = blk
        b = jnp.cumsum(gb, axis=2)
        o_inter = jnp.einsum('bhtk,bhkv->bhtv', qb * jnp.exp(b), S)
        diff = jnp.where(causal[:, :, None], b[:, :, :, None, :] - b[:, :, None, :, :], -jnp.inf)
        att = jnp.einsum('bhtk,bhsk,bhtsk->bhts', qb, kb, jnp.exp(diff))
        o = o_inter + jnp.einsum('bhts,bhsv->bhtv', att, vb)
        b_last = b[:, :, -1:, :]
        S = jnp.exp(b_last[:, :, 0, :, None]) * S + jnp.einsum('bhsk,bhsv->bhkv', kb * jnp.exp(b_last - b), vb)
        return S, o

    S, o = lax.scan(step, s0, tuple(to_blocks(a) for a in (q, k, v, g)))
    o = jnp.transpose(o, (1, 0, 3, 2, 4)).reshape(B, n * HG_BLOCK, q.shape[2], v.shape[-1])[:, :T]
    return o, S


def layer_step(x, c, past, li, lb, bias_table, w_ada, b_ada, g_ffn1, w_ffn1_up, w_ffn1_down, g_mix, w_in, b_fox_f,
               g_hg_o, g_diff_q, g_diff_k, diff_lambda, g_diff_o, g_fox_q, g_fox_k, w_branch, w_out,
               g_ffn2, w_ffn2_up, w_ffn2_down):
    f32 = jnp.float32
    B, T, _ = x.shape
    past_len = 0 if past is None else past[0].shape[1]
    q_pos = past_len + jnp.arange(T, dtype=jnp.int32)
    k_pos = jnp.arange(past_len + T, dtype=jnp.int32)

    mod = jax.nn.silu(c) @ w_ada + b_ada
    sh1, sc1, gt1, sh2, sc2, gt2, sh3, sc3, gt3 = jnp.split(mod[:, None, :], N_ADA, axis=-1)

    h = rms_norm(x, g_ffn1) * (1.0 + sc1) + sh1
    x = x + 0.5 * gt1 * swiglu(h, w_ffn1_up, w_ffn1_down)

    h = rms_norm(x, g_mix) * (1.0 + sc2) + sh2
    (hq, hf, hi, hg, dq, dk, dv, fq, fk, fv, ff, gates) = jnp.split(h @ w_in, IN_SPLITS, axis=-1)

    hs = (B, T, HG_HEADS, HEAD_DIM)
    q_hg = jax.nn.silu(hq).reshape(hs).astype(f32) * (HEAD_DIM ** -0.5)
    fl = hf.reshape(hs).astype(f32)
    lbh = lb.reshape(HG_HEADS, HEAD_DIM)
    log_f = jnp.logaddexp(jnp.log(lbh), jnp.log1p(-lbh) + jax.nn.log_sigmoid(fl))
    k_hg = (1.0 - lbh) * jax.nn.sigmoid(-fl)
    s0 = jnp.zeros((B, HG_HEADS, HEAD_DIM, HEAD_DIM), f32) if past is None else past[5].astype(f32)
    o_hg, s_hg = hgrn2_scan(q_hg, k_hg, hi.reshape(hs).astype(f32), log_f, s0)
    o_hg = rms_norm(o_hg.astype(x.dtype), g_hg_o) * jax.nn.sigmoid(hg.reshape(hs))

    dq = rms_norm(dq.reshape(B, T, DIFF_HEADS, 2, HEAD_DIM), g_diff_q)
    dk = rms_norm(dk.reshape(B, T, DIFF_HEADS, 2, HEAD_DIM), g_diff_k)
    dv = dv.reshape(B, T, DIFF_HEADS, 2 * HEAD_DIM)
    lam_init = 0.8 - 0.6 * math.exp(-0.3 * li)
    dl = diff_lambda.astype(f32)
    lam = jnp.exp(jnp.sum(dl[0] * dl[1])) - jnp.exp(jnp.sum(dl[2] * dl[3])) + lam_init
    dk_all = dk if past is None else jnp.concatenate([past[0], dk], axis=1)
    dv_all = dv if past is None else jnp.concatenate([past[1], dv], axis=1)
    o_diff = sweep_queries(
        lambda qb, pb: diff_core(qb, pb, dk_all, dv_all, k_pos, bias_table, lam, lam_init, g_diff_o), (dq,), q_pos)

    fq = rms_norm(fq.reshape(B, T, FOX_HEADS, HEAD_DIM), g_fox_q)
    fk = rms_norm(fk.reshape(B, T, FOX_HEADS, HEAD_DIM), g_fox_k)
    fv = fv.reshape(B, T, FOX_HEADS, HEAD_DIM)
    logf = jax.nn.log_sigmoid(ff.astype(f32) + b_fox_f.astype(f32))
    logf_all = logf if past is None else jnp.concatenate([past[4].astype(f32), logf], axis=1)
    cum = jnp.cumsum(logf_all, axis=1)
    fk_all = fk if past is None else jnp.concatenate([past[2], fk], axis=1)
    fv_all = fv if past is None else jnp.concatenate([past[3], fv], axis=1)
    o_fox = sweep_queries(
        lambda qb, cb, pb: fox_core(qb, cb, pb, fk_all, fv_all, cum, k_pos), (fq, cum[:, past_len:]), q_pos)

    branches = jnp.stack([o_hg.reshape(B, T, BRANCH_WIDTH), o_diff.reshape(B, T, BRANCH_WIDTH),
                          o_fox.reshape(B, T, BRANCH_WIDTH)], axis=2)
    up = jnp.einsum('btnm,nmd->btnd', branches, w_branch)
    gate = jax.nn.sigmoid(gates.reshape(B, T, N_BRANCH, D_MODEL))
    x = x + gt2 * (jnp.sum(gate * up, axis=2) @ w_out)

    h = rms_norm(x, g_ffn2) * (1.0 + sc3) + sh3
    x = x + 0.5 * gt3 * swiglu(h, w_ffn2_up, w_ffn2_down)

    new_state = (dk, dv, fk, fv, logf.astype(x.dtype), s_hg.astype(x.dtype))
    return x, new_state


def setup_inputs(seed: int = 0) -> dict:
    key = jax.random.key(seed)
    ks = iter(jax.random.split(key, 40))
    f32 = jnp.float32
    D = D_MODEL

    def nrm(shape, scale=1.0):
        return jax.random.normal(next(ks), shape, f32) * scale

    def gain(shape):
        return 1.0 + nrm(shape, 0.02)

    return {
        'x_prompt': nrm((BATCH, SEQ, D)),
        'x_sample': nrm((DEC_BATCH, DEC_SEQ, D)),
        'c_prompt': nrm((BATCH, D)),
        'c_sample': nrm((DEC_BATCH, D)),
        'cache_diff_k': nrm((DEPTH, DEC_BATCH, PAST_LEN, DIFF_HEADS, 2, HEAD_DIM)),
        'cache_diff_v': nrm((DEPTH, DEC_BATCH, PAST_LEN, DIFF_HEADS, 2 * HEAD_DIM)),
        'cache_fox_k': nrm((DEPTH, DEC_BATCH, PAST_LEN, FOX_HEADS, HEAD_DIM)),
        'cache_fox_v': nrm((DEPTH, DEC_BATCH, PAST_LEN, FOX_HEADS, HEAD_DIM)),
        'cache_fox_logf': jax.nn.log_sigmoid(2.5 + nrm((DEPTH, DEC_BATCH, PAST_LEN, FOX_HEADS))),
        'state_hgrn': nrm((DEPTH, DEC_BATCH, HG_HEADS, HEAD_DIM, HEAD_DIM), 0.5),
        'rel_bias_table': nrm((NUM_BUCKETS, DIFF_HEADS), 0.5),
        'hgrn_lb_logits': nrm((DEPTH, BRANCH_WIDTH)),
        'w_ada': nrm((DEPTH, D, N_ADA * D), 0.5 * D ** -0.5),
        'b_ada': nrm((DEPTH, N_ADA * D), 0.02),
        'g_ffn1': gain((DEPTH, D)),
        'w_ffn1_up': nrm((DEPTH, D, 2 * D_FF), D ** -0.5),
        'w_ffn1_down': nrm((DEPTH, D_FF, D), D_FF ** -0.5),
        'g_mix': gain((DEPTH, D)),
        'w_in': nrm((DEPTH, D, N_IN), D ** -0.5),
        'b_fox_f': 2.5 + nrm((DEPTH, FOX_HEADS), 0.5),
        'g_hg_o': gain((DEPTH, HEAD_DIM)),
        'g_diff_q': gain((DEPTH, HEAD_DIM)),
        'g_diff_k': gain((DEPTH, HEAD_DIM)),
        'diff_lambda': nrm((DEPTH, 4, HEAD_DIM), 0.1),
        'g_diff_o': gain((DEPTH, 2 * HEAD_DIM)),
        'g_fox_q': gain((DEPTH, HEAD_DIM)),
        'g_fox_k': gain((DEPTH, HEAD_DIM)),
        'w_branch': nrm((DEPTH, N_BRANCH, BRANCH_WIDTH, D), BRANCH_WIDTH ** -0.5),
        'w_out': nrm((DEPTH, D, D), D ** -0.5),
        'g_ffn2': gain((DEPTH, D)),
        'w_ffn2_up': nrm((DEPTH, D, 2 * D_FF), D ** -0.5),
        'w_ffn2_down': nrm((DEPTH, D_FF, D), D_FF ** -0.5),
    }


def reference(x_prompt, x_sample, c_prompt, c_sample, cache_diff_k, cache_diff_v, cache_fox_k, cache_fox_v,
              cache_fox_logf, state_hgrn, rel_bias_table, hgrn_lb_logits, w_ada, b_ada, g_ffn1, w_ffn1_up,
              w_ffn1_down, g_mix, w_in, b_fox_f, g_hg_o, g_diff_q, g_diff_k, diff_lambda, g_diff_o, g_fox_q,
              g_fox_k, w_branch, w_out, g_ffn2, w_ffn2_up, w_ffn2_down):
    lbs = hgrn_lower_bounds(hgrn_lb_logits)
    y_prompt, y_sample = x_prompt, x_sample
    new_p, new_s = [], []
    for li in range(DEPTH):
        lw = (w_ada[li], b_ada[li], g_ffn1[li], w_ffn1_up[li], w_ffn1_down[li], g_mix[li], w_in[li], b_fox_f[li],
              g_hg_o[li], g_diff_q[li], g_diff_k[li], diff_lambda[li], g_diff_o[li], g_fox_q[li], g_fox_k[li],
              w_branch[li], w_out[li], g_ffn2[li], w_ffn2_up[li], w_ffn2_down[li])
        y_prompt, st_p = layer_step(y_prompt, c_prompt, None, li, lbs[li], rel_bias_table, *lw)
        past = (cache_diff_k[li], cache_diff_v[li], cache_fox_k[li], cache_fox_v[li], cache_fox_logf[li], state_hgrn[li])
        y_sample, st_s = layer_step(y_sample, c_sample, past, li, lbs[li], rel_bias_table, *lw)
        new_p.append(st_p)
        new_s.append(st_s)
    p_diff_k = jnp.stack([s[0] for s in new_p])
    p_diff_v = jnp.stack([s[1] for s in new_p])
    p_fox_k = jnp.stack([s[2] for s in new_p])
    p_fox_v = jnp.stack([s[3] for s in new_p])
    p_fox_logf = jnp.stack([s[4] for s in new_p])
    p_hgrn = jnp.stack([s[5] for s in new_p])
    s_diff_k = jnp.stack([s[0] for s in new_s])
    s_diff_v = jnp.stack([s[1] for s in new_s])
    s_fox_k = jnp.stack([s[2] for s in new_s])
    s_fox_v = jnp.stack([s[3] for s in new_s])
    s_fox_logf = jnp.stack([s[4] for s in new_s])
    s_hgrn = jnp.stack([s[5] for s in new_s])
    return (y_prompt, y_sample, p_diff_k, p_diff_v, p_fox_k, p_fox_v, p_fox_logf, p_hgrn,
            s_diff_k, s_diff_v, s_fox_k, s_fox_v, s_fox_logf, s_hgrn)
```

```python
import functools
import math
from typing import NamedTuple

import jax
import jax.numpy as jnp
import numpy as np
from jax import lax
from jax.experimental import pallas as pl
from jax.experimental.pallas import tpu as pltpu

F32 = jnp.float32
BF16 = jnp.bfloat16

HEAD_DIM = 128
SEG = 64
CHUNK = 64
NUM_BUCKETS = 32
MAX_DISTANCE = 128
N_ADA = 9
N_BRANCH = 3
EPS = 1e-6
NEG = -1e30
HG_CHUNK = 128
HG_SUB = 16
VMEM_LIMIT_BYTES = 56 * 1024 * 1024


class Cfg(NamedTuple):
    d_model: int
    batch: int
    seq: int
    depth: int
    dec_batch: int
    dec_seq: int
    past_len: int
    tm: int = 512
    tn: int = 512
    tq: int = 256
    tkc: int = 512
    t_ada: int = 1024

    @property
    def width(self):
        return self.d_model // 2

    @property
    def hg_heads(self):
        return self.width // HEAD_DIM

    @property
    def diff_heads(self):
        return self.width // (2 * HEAD_DIM)

    @property
    def fox_heads(self):
        return self.width // HEAD_DIM

    @property
    def d_ff(self):
        return ((8 * self.d_model // 3 + 127) // 128) * 128

    @property
    def d_ff_pad(self):
        return ((self.d_ff + self.tn - 1) // self.tn) * self.tn

    @property
    def nt_prompt(self):
        return self.batch * self.seq

    @property
    def nt_sample(self):
        return self.dec_batch * self.dec_seq

    @property
    def nt(self):
        return self.nt_prompt + self.nt_sample


FULL_CFG = Cfg(d_model=2048, batch=2, seq=4096, depth=4, dec_batch=32, dec_seq=64, past_len=2048)


def _params(*sem):
    return pltpu.CompilerParams(dimension_semantics=sem, vmem_limit_bytes=VMEM_LIMIT_BYTES)


def _silu(x):
    return x * jax.nn.sigmoid(x)


def _log_sigmoid(x):
    return jnp.minimum(x, 0.0) - jnp.log1p(jnp.exp(-jnp.abs(x)))


def _nt_dot(a, b):
    return lax.dot_general(a, b, (((1,), (1,)), ((), ())), preferred_element_type=F32)


def _tn_dot(a, b):
    return lax.dot_general(a, b, (((0,), (0,)), ((), ())), preferred_element_type=F32)


def _ada_kernel(c_ref, w_ref, b_ref, o_ref):
    a = _silu(c_ref[...]).astype(BF16)
    o_ref[...] = jnp.dot(a, w_ref[...].astype(BF16), preferred_element_type=F32) + b_ref[...]


def ada_mods(c_all, w_ada, b_ada, tn):
    depth, d, n = w_ada.shape
    r = c_all.shape[0]
    return pl.pallas_call(
        _ada_kernel,
        grid=(depth, n // tn),
        in_specs=[pl.BlockSpec((r, d), lambda l, j: (0, 0)),
                  pl.BlockSpec((None, d, tn), lambda l, j: (l, 0, j)),
                  pl.BlockSpec((None, 1, tn), lambda l, j: (l, 0, j))],
        out_specs=pl.BlockSpec((None, r, tn), lambda l, j: (l, 0, j)),
        out_shape=jax.ShapeDtypeStruct((depth, r, n), F32),
        compiler_params=_params("parallel", "parallel"),
        name="ada_mods",
    )(c_all, w_ada, b_ada.reshape(depth, 1, n))


def _norm_mod_kernel(x_ref, g_ref, sc_ref, sh_ref, o_ref):
    x = x_ref[...]
    ms = jnp.mean(x * x, axis=-1, keepdims=True)
    y = x * lax.rsqrt(ms + EPS) * g_ref[...]
    y = y * (1.0 + sc_ref[...]) + sh_ref[...]
    o_ref[...] = y.reshape(o_ref.shape).astype(BF16)


def norm_mod(x3, g, mods, i_scale, i_shift, tm):
    ns, seg, d = x3.shape
    s = tm // seg
    return pl.pallas_call(
        _norm_mod_kernel,
        grid=(ns // s,),
        in_specs=[pl.BlockSpec((s, seg, d), lambda i: (i, 0, 0)),
                  pl.BlockSpec((1, d), lambda i: (0, 0)),
                  pl.BlockSpec((None, s, 1, d), lambda i: (i_scale, i, 0, 0)),
                  pl.BlockSpec((None, s, 1, d), lambda i: (i_shift, i, 0, 0))],
        out_specs=pl.BlockSpec((tm, d), lambda i: (i, 0)),
        out_shape=jax.ShapeDtypeStruct((ns * seg, d), BF16),
        compiler_params=_params("parallel"),
        name="norm_mod",
    )(x3, g.reshape(1, d), mods, mods)


def _ffn_up_kernel(h_ref, wa_ref, wb_ref, o_ref):
    h = h_ref[...]
    a = jnp.dot(h, wa_ref[...], preferred_element_type=F32)
    b = jnp.dot(h, wb_ref[...], preferred_element_type=F32)
    o_ref[...] = (_silu(a) * b).astype(BF16)


def ffn_up(h, wa, wb, tm, tn):
    m, k = h.shape
    f = wa.shape[1]
    return pl.pallas_call(
        _ffn_up_kernel,
        grid=(m // tm, f // tn),
        in_specs=[pl.BlockSpec((tm, k), lambda i, j: (i, 0)),
                  pl.BlockSpec((k, tn), lambda i, j: (0, j)),
                  pl.BlockSpec((k, tn), lambda i, j: (0, j))],
        out_specs=pl.BlockSpec((tm, tn), lambda i, j: (i, j)),
        out_shape=jax.ShapeDtypeStruct((m, f), BF16),
        compiler_params=_params("parallel", "arbitrary"),
        name="ffn_up",
    )(h, wa, wb)


def _mm_residual_kernel(a_ref, w_ref, x_ref, gt_ref, o_ref, *, coef):
    acc = jnp.dot(a_ref[...], w_ref[...], preferred_element_type=F32)
    o_ref[...] = x_ref[...] + (coef * gt_ref[...]) * acc.reshape(x_ref.shape)


def mm_residual(a, w, x3, mods, i_gate, coef, tm, tn):
    m, k = a.shape
    n = w.shape[1]
    ns, seg, _ = x3.shape
    s = tm // seg
    return pl.pallas_call(
        functools.partial(_mm_residual_kernel, coef=coef),
        grid=(m // tm, n // tn),
        in_specs=[pl.BlockSpec((tm, k), lambda i, j: (i, 0)),
                  pl.BlockSpec((k, tn), lambda i, j: (0, j)),
                  pl.BlockSpec((s, seg, tn), lambda i, j: (i, 0, j)),
                  pl.BlockSpec((None, s, 1, tn), lambda i, j: (i_gate, i, 0, j))],
        out_specs=pl.BlockSpec((s, seg, tn), lambda i, j: (i, 0, j)),
        out_shape=jax.ShapeDtypeStruct(x3.shape, F32),
        compiler_params=_params("parallel", "arbitrary"),
        name="mm_residual",
    )(a, w, x3, mods)


def _proj_hgrn_kernel(h_ref, w_ref, la_ref, l1m_ref, oml_ref, q_ref, g_ref, k_ref, v_ref, og_ref):
    j = pl.program_id(1)
    acc = jnp.dot(h_ref[...], w_ref[...], preferred_element_type=F32)

    @pl.when(j == 0)
    def _():
        q_ref[...] = (_silu(acc) * (HEAD_DIM ** -0.5)).astype(BF16)

    @pl.when(j == 1)
    def _():
        la = la_ref[...]
        y = l1m_ref[...] + _log_sigmoid(acc)
        g_ref[...] = jnp.maximum(la, y) + jnp.log1p(jnp.exp(-jnp.abs(la - y)))
        k_ref[...] = (oml_ref[...] * jax.nn.sigmoid(-acc)).astype(BF16)

    @pl.when(j == 2)
    def _():
        v_ref[...] = acc.astype(BF16)

    @pl.when(j == 3)
    def _():
        og_ref[...] = jax.nn.sigmoid(acc).astype(BF16)


def proj_hgrn(h, w_main, log_lb, log1m_lb, one_m_lb, tm):
    m, k = h.shape
    w = log_lb.shape[-1]
    row = lambda i, j: (i, 0)
    vec = pl.BlockSpec((1, w), lambda i, j: (0, 0))
    return pl.pallas_call(
        _proj_hgrn_kernel,
        grid=(m // tm, 4),
        in_specs=[pl.BlockSpec((tm, k), row),
                  pl.BlockSpec((k, w), lambda i, j: (0, j)),
                  vec, vec, vec],
        out_specs=[pl.BlockSpec((tm, w), row)] * 5,
        out_shape=[jax.ShapeDtypeStruct((m, w), BF16),
                   jax.ShapeDtypeStruct((m, w), F32),
                   jax.ShapeDtypeStruct((m, w), BF16),
                   jax.ShapeDtypeStruct((m, w), BF16),
                   jax.ShapeDtypeStruct((m, w), BF16)],
        compiler_params=_params("parallel", "arbitrary"),
        name="proj_hgrn",
    )(h, w_main, log_lb, log1m_lb, one_m_lb)


def _head_rms(acc, g):
    outs = []
    for c in range(acc.shape[1] // HEAD_DIM):
        ch = acc[:, c * HEAD_DIM:(c + 1) * HEAD_DIM]
        ms = jnp.mean(ch * ch, axis=-1, keepdims=True)
        outs.append(ch * lax.rsqrt(ms + EPS) * g)
    return outs


def _proj_attn_kernel(*refs, q_scale, with_forget):
    if with_forget:
        (h_ref, w_ref, gq_ref, gk_ref, wf_ref, bf_ref,
         qb_ref, kf_ref, kb_ref, vf_ref, vb_ref, lf_ref) = refs
    else:
        h_ref, w_ref, gq_ref, gk_ref, qb_ref, kf_ref, kb_ref, vf_ref, vb_ref = refs
    j = pl.program_id(1)
    acc = jnp.dot(h_ref[...], w_ref[...], preferred_element_type=F32)

    @pl.when(j == 0)
    def _():
        for c, ch in enumerate(_head_rms(acc, gq_ref[...])):
            qb_ref[:, c * HEAD_DIM:(c + 1) * HEAD_DIM] = (ch * q_scale).astype(BF16)
        if with_forget:
            f = jnp.dot(h_ref[...], wf_ref[...], preferred_element_type=F32) + bf_ref[...]
            lf_ref[...] = _log_sigmoid(f)

    @pl.when(j == 1)
    def _():
        for c, ch in enumerate(_head_rms(acc, gk_ref[...])):
            kf_ref[:, c * HEAD_DIM:(c + 1) * HEAD_DIM] = ch
            kb_ref[:, c * HEAD_DIM:(c + 1) * HEAD_DIM] = ch.astype(BF16)

    @pl.when(j == 2)
    def _():
        vf_ref[...] = acc
        vb_ref[...] = acc.astype(BF16)


def proj_attn(h, w_main, col_block, g_q, g_k, tm, w_forget=None, b_forget=None):
    m, k = h.shape
    w = w_main.shape[1] // 10
    with_forget = w_forget is not None
    row = lambda i, j: (i, 0)
    vec = pl.BlockSpec((1, HEAD_DIM), lambda i, j: (0, 0))
    in_specs = [pl.BlockSpec((tm, k), row),
                pl.BlockSpec((k, w), lambda i, j: (0, col_block + j)),
                vec, vec]
    args = [h, w_main, g_q.reshape(1, HEAD_DIM), g_k.reshape(1, HEAD_DIM)]
    out_specs = [pl.BlockSpec((tm, w), row)] * 5
    out_shape = [jax.ShapeDtypeStruct((m, w), BF16),
                 jax.ShapeDtypeStruct((m, w), F32), jax.ShapeDtypeStruct((m, w), BF16),
                 jax.ShapeDtypeStruct((m, w), F32), jax.ShapeDtypeStruct((m, w), BF16)]
    if with_forget:
        in_specs += [pl.BlockSpec((k, HEAD_DIM), lambda i, j: (0, 0)), vec]
        args += [w_forget, b_forget]
        out_specs.append(pl.BlockSpec((tm, HEAD_DIM), row))
        out_shape.append(jax.ShapeDtypeStruct((m, HEAD_DIM), F32))
    return pl.pallas_call(
        functools.partial(_proj_attn_kernel, q_scale=HEAD_DIM ** -0.5, with_forget=with_forget),
        grid=(m // tm, 3),
        in_specs=in_specs, out_specs=out_specs, out_shape=out_shape,
        compiler_params=_params("parallel", "arbitrary"),
        name="proj_fox" if with_forget else "proj_diff",
    )(*args)


def _proj_gates_kernel(h_ref, w_ref, o_ref):
    acc = jnp.dot(h_ref[...], w_ref[...], preferred_element_type=F32)
    o_ref[...] = jax.nn.sigmoid(acc).astype(BF16)


def proj_gates(h, w_gates, tm, tn):
    m, k = h.shape
    n = w_gates.shape[1]
    return pl.pallas_call(
        _proj_gates_kernel,
        grid=(m // tm, n // tn),
        in_specs=[pl.BlockSpec((tm, k), lambda i, j: (i, 0)),
                  pl.BlockSpec((k, tn), lambda i, j: (0, j))],
        out_specs=pl.BlockSpec((tm, tn), lambda i, j: (i, j)),
        out_shape=jax.ShapeDtypeStruct((m, n), BF16),
        compiler_params=_params("parallel", "arbitrary"),
        name="proj_gates",
    )(h, w_gates)


def _merge_kernel(b0_ref, b1_ref, b2_ref, g0_ref, g1_ref, g2_ref, w_ref, o_ref):
    acc = None
    for n, (b_ref, g_ref) in enumerate(((b0_ref, g0_ref), (b1_ref, g1_ref), (b2_ref, g2_ref))):
        up = jnp.dot(b_ref[...], w_ref[n], preferred_element_type=F32)
        t = g_ref[...].astype(F32) * up
        acc = t if acc is None else acc + t
    o_ref[...] = acc.astype(BF16)


def merge_branches(branches, gates, w_branch, tm, tn):
    m, w = branches[0].shape
    d = w_branch.shape[2]
    nj = d // tn
    bspec = pl.BlockSpec((tm, w), lambda i, j: (i, 0))
    gspecs = [pl.BlockSpec((tm, tn), functools.partial(lambda i, j, n: (i, n * nj + j), n=n))
              for n in range(N_BRANCH)]
    return pl.pallas_call(
        _merge_kernel,
        grid=(m // tm, nj),
        in_specs=[bspec, bspec, bspec] + gspecs +
                 [pl.BlockSpec((N_BRANCH, w, tn), lambda i, j: (0, 0, j))],
        out_specs=pl.BlockSpec((tm, tn), lambda i, j: (i, j)),
        out_shape=jax.ShapeDtypeStruct((m, d), BF16),
        compiler_params=_params("parallel", "arbitrary"),
        name="merge_branches",
    )(*branches, gates, gates, gates, w_branch)


def _split3_dot(tri, g):
    g1 = g.astype(BF16)
    r1 = g - g1.astype(F32)
    g2 = r1.astype(BF16)
    g3 = (r1 - g2.astype(F32)).astype(BF16)
    return (jnp.dot(tri, g1, preferred_element_type=F32)
            + jnp.dot(tri, g2, preferred_element_type=F32)
            + jnp.dot(tri, g3, preferred_element_type=F32))


def _hgrn_kernel(q_ref, k_ref, v_ref, g_ref, og_ref, s0_ref, gn_ref, o_ref, sout_ref, st_ref,
                 *, heads, rows):
    c = pl.program_id(1)
    last = pl.num_programs(1) - 1
    C, U = HG_CHUNK, HG_SUB
    nsub = C // U

    @pl.when(c == 0)
    def _():
        for h in range(heads):
            st_ref[h] = s0_ref[h].T

    ri = lax.broadcasted_iota(jnp.int32, (C, C), 0)
    ci = lax.broadcasted_iota(jnp.int32, (C, C), 1)
    tri = (ri >= ci).astype(BF16)
    ti = lax.broadcasted_iota(jnp.int32, (U, U, 1), 0)
    si = lax.broadcasted_iota(jnp.int32, (U, U, 1), 1)
    causal = ti >= si
    ones = jnp.ones((HEAD_DIM, HEAD_DIM), BF16)
    gn = gn_ref[...]

    def pad_rows(x):
        if rows == C:
            return x
        return jnp.concatenate([x, jnp.zeros((C - rows, x.shape[1]), x.dtype)], axis=0)

    def head(h, carry):
        sl = pl.ds(pl.multiple_of(h * HEAD_DIM, HEAD_DIM), HEAD_DIM)
        q = pad_rows(q_ref[:, sl].astype(F32))
        k = pad_rows(k_ref[:, sl].astype(F32))
        v = pad_rows(v_ref[:, sl])
        g = pad_rows(g_ref[:, sl])
        b = _split3_dot(tri, g)
        st = st_ref[h]
        o = _nt_dot((q * jnp.exp(b)).astype(BF16), st.astype(BF16))
        b_last = b[C - 1:C, :]
        kd = (k * jnp.exp(b_last - b)).astype(BF16)
        st_ref[h] = st * jnp.exp(b_last) + _tn_dot(v, kd)
        vf = v.astype(F32)
        att_rows = []
        o_diag = []
        for i in range(nsub):
            lo, hi = i * U, (i + 1) * U
            bi, qi, ki = b[lo:hi], q[lo:hi], k[lo:hi]
            if i == 0:
                att_rows.append(jnp.zeros((U, C), BF16))
            else:
                r = b[lo - 1:lo, :]
                q_dec = (qi * jnp.exp(bi - r)).astype(BF16)
                k_dec = (k[:lo] * jnp.exp(r - b[:lo])).astype(BF16)
                k_dec = jnp.concatenate([k_dec, jnp.zeros((C - lo, HEAD_DIM), BF16)], axis=0)
                att_rows.append(_nt_dot(q_dec, k_dec).astype(BF16))
            diff = bi[:, None, :] - bi[None, :, :]
            e = jnp.exp(jnp.where(causal, diff, NEG))
            p = (qi[:, None, :] * ki[None, :, :]) * e
            att_b = jnp.dot(p.reshape(U * U, HEAD_DIM).astype(BF16), ones,
                            preferred_element_type=F32)
            o_diag.append(jnp.sum(att_b.reshape(U, U, HEAD_DIM) * vf[lo:hi][None, :, :], axis=1))
        att = jnp.concatenate(att_rows, axis=0)
        o = o + jnp.dot(att, v, preferred_element_type=F32) + jnp.concatenate(o_diag, axis=0)
        o = o[:rows]
        ms = jnp.mean(o * o, axis=-1, keepdims=True)
        o = o * lax.rsqrt(ms + EPS) * gn * og_ref[:, sl].astype(F32)
        o_ref[:, sl] = o.astype(BF16)
        return carry

    lax.fori_loop(0, heads, head, 0)

    @pl.when(c == last)
    def _():
        for h in range(heads):
            sout_ref[h] = st_ref[h].T


def hgrn_mixer(q, k, v, g, og, s0, g_norm, row0, nb, t):
    w = q.shape[1]
    heads = w // HEAD_DIM
    rows = min(t, HG_CHUNK)
    nc = t // rows
    blk0 = row0 // rows
    tok = pl.BlockSpec((rows, w), lambda b, c: (blk0 + b * nc + c, 0))
    state = pl.BlockSpec((None, heads, HEAD_DIM, HEAD_DIM), lambda b, c: (b, 0, 0, 0))
    return pl.pallas_call(
        functools.partial(_hgrn_kernel, heads=heads, rows=rows),
        grid=(nb, nc),
        in_specs=[tok, tok, tok, tok, tok, state, pl.BlockSpec((1, HEAD_DIM), lambda b, c: (0, 0))],
        out_specs=[pl.BlockSpec((rows, w), lambda b, c: (b * nc + c, 0)), state],
        out_shape=[jax.ShapeDtypeStruct((nb * t, w), BF16),
                   jax.ShapeDtypeStruct((nb, heads, HEAD_DIM, HEAD_DIM), F32)],
        scratch_shapes=[pltpu.VMEM((heads, HEAD_DIM, HEAD_DIM), F32)],
        compiler_params=_params("parallel", "arbitrary"),
        name="hgrn_mixer",
    )(q, k, v, g, og, s0, g_norm.reshape(1, HEAD_DIM))


def _softmax_step(s, v, m_ref, l_ref, acc_ref, idx):
    m_old = m_ref[idx]
    m_new = jnp.maximum(m_old, jnp.max(s, axis=-1, keepdims=True))
    alpha = jnp.exp(m_old - m_new)
    p = jnp.exp(s - m_new)
    l_ref[idx] = alpha * l_ref[idx] + jnp.sum(p, axis=-1, keepdims=True)
    acc_ref[idx] = alpha * acc_ref[idx] + jnp.dot(p.astype(BF16), v, preferred_element_type=F32)
    m_ref[idx] = m_new


def _softmax_init(m_ref, l_ref, acc_ref):
    m_ref[...] = jnp.full(m_ref.shape, NEG, F32)
    l_ref[...] = jnp.zeros(l_ref.shape, F32)
    acc_ref[...] = jnp.zeros(acc_ref.shape, F32)


def _diff_finish(acc_ref, l_ref, h, lam, g_o, out_scale):
    o = acc_ref[2 * h] / l_ref[2 * h] - lam * (acc_ref[2 * h + 1] / l_ref[2 * h + 1])
    ms = jnp.mean(o * o, axis=-1, keepdims=True)
    return (o * lax.rsqrt(ms + EPS) * g_o * out_scale).astype(BF16)


def _fox_prompt_kernel(q_ref, k_ref, v_ref, ck_ref, o_ref, m_ref, l_ref, acc_ref, *, t):
    qi = pl.program_id(2)
    q = q_ref[...]
    _softmax_init(m_ref, l_ref, acc_ref)

    def tile(kt, mask):
        rows = pl.ds(pl.multiple_of(kt * t, t), t)
        s = _nt_dot(q, k_ref[rows, :]) - ck_ref[kt]
        if mask is not None:
            s = jnp.where(mask, s, NEG)
        _softmax_step(s, v_ref[rows, :], m_ref, l_ref, acc_ref, 0)

    def body(kt, carry):
        tile(kt, None)
        return carry

    lax.fori_loop(0, qi, body, 0)
    ri = lax.broadcasted_iota(jnp.int32, (t, t), 0)
    ci = lax.broadcasted_iota(jnp.int32, (t, t), 1)
    tile(qi, ci <= ri)
    o_ref[...] = (acc_ref[0] / l_ref[0]).astype(BF16)


def fox_prompt(q, k, v, ck, nb, seq, t):
    heads = q.shape[1] // HEAD_DIM
    nq = seq // t
    kv = pl.BlockSpec((seq, HEAD_DIM), lambda b, h, i: (b, h))
    return pl.pallas_call(
        functools.partial(_fox_prompt_kernel, t=t),
        grid=(nb, heads, nq),
        in_specs=[pl.BlockSpec((t, HEAD_DIM), lambda b, h, i: (b * nq + i, h)), kv, kv,
                  pl.BlockSpec((None, None, nq, 1, t), lambda b, h, i: (b, h, 0, 0, 0))],
        out_specs=pl.BlockSpec((t, HEAD_DIM), lambda b, h, i: (b * nq + i, h)),
        out_shape=jax.ShapeDtypeStruct((nb * seq, heads * HEAD_DIM), BF16),
        scratch_shapes=[pltpu.VMEM((1, t, 1), F32), pltpu.VMEM((1, t, 1), F32),
                        pltpu.VMEM((1, t, HEAD_DIM), F32)],
        compiler_params=_params("parallel", "parallel", "arbitrary"),
        name="fox_prompt",
    )(q, k, v, ck)


def _fox_sample_kernel(q_ref, kc_ref, vc_ref, ckc_ref, kn_ref, vn_ref, ckn_ref, o_ref,
                       m_ref, l_ref, acc_ref, *, heads):
    kt = pl.program_id(1)
    last = pl.num_programs(1) - 1
    tq = q_ref.shape[0]

    @pl.when(kt == 0)
    def _():
        _softmax_init(m_ref, l_ref, acc_ref)

    for h in range(heads):
        cols = slice(h * HEAD_DIM, (h + 1) * HEAD_DIM)
        s = _nt_dot(q_ref[:, cols], kc_ref[:, cols].astype(BF16)) - ckc_ref[h:h + 1, :]
        _softmax_step(s, vc_ref[:, cols].astype(BF16), m_ref, l_ref, acc_ref, h)

    @pl.when(kt == last)
    def _():
        ri = lax.broadcasted_iota(jnp.int32, (tq, tq), 0)
        ci = lax.broadcasted_iota(jnp.int32, (tq, tq), 1)
        for h in range(heads):
            cols = slice(h * HEAD_DIM, (h + 1) * HEAD_DIM)
            s = _nt_dot(q_ref[:, cols], kn_ref[:, cols]) - ckn_ref[h:h + 1, :]
            s = jnp.where(ci <= ri, s, NEG)
            _softmax_step(s, vn_ref[:, cols], m_ref, l_ref, acc_ref, h)
            o_ref[:, cols] = (acc_ref[h] / l_ref[h]).astype(BF16)


def fox_sample(q, k, v, cache_k, cache_v, layer, ck_cache, ck_new, row0, nb, tq, tk):
    w = q.shape[1]
    heads = w // HEAD_DIM
    past = cache_k.shape[2]
    nkt = past // tk
    blk0 = row0 // tq
    new = pl.BlockSpec((tq, w), lambda b, j: (blk0 + b, 0))
    cache = pl.BlockSpec((None, None, tk, w), lambda b, j: (layer, b, j, 0))
    return pl.pallas_call(
        functools.partial(_fox_sample_kernel, heads=heads),
        grid=(nb, nkt),
        in_specs=[new, cache, cache,
                  pl.BlockSpec((None, heads, tk), lambda b, j: (b, 0, j)),
                  new, new,
                  pl.BlockSpec((None, heads, tq), lambda b, j: (b, 0, 0))],
        out_specs=pl.BlockSpec((tq, w), lambda b, j: (b, 0)),
        out_shape=jax.ShapeDtypeStruct((nb * tq, w), BF16),
        scratch_shapes=[pltpu.VMEM((heads, tq, 1), F32), pltpu.VMEM((heads, tq, 1), F32),
                        pltpu.VMEM((heads, tq, HEAD_DIM), F32)],
        compiler_params=_params("parallel", "arbitrary"),
        name="fox_sample",
    )(q, cache_k, cache_v, ck_cache, k, v, ck_new)


def _diff_prompt_kernel(q_ref, k_ref, v_ref, b0_ref, b1_ref, lam_ref, go_ref, o_ref,
                        m_ref, l_ref, acc_ref, *, t, out_scale):
    qi = pl.program_id(2)
    _softmax_init(m_ref, l_ref, acc_ref)

    def tile(kt, bias_ref):
        rows = pl.ds(pl.multiple_of(kt * t, t), t)
        v = v_ref[rows, :]
        for c in range(2):
            cols = slice(c * HEAD_DIM, (c + 1) * HEAD_DIM)
            s = _nt_dot(q_ref[:, cols], k_ref[rows, cols])
            if bias_ref is not None:
                s = s + bias_ref[...]
            _softmax_step(s, v, m_ref, l_ref, acc_ref, c)

    def body(kt, carry):
        tile(kt, None)
        return carry

    lax.fori_loop(0, jnp.maximum(qi - 1, 0), body, 0)

    @pl.when(qi >= 1)
    def _():
        tile(qi - 1, b1_ref)

    tile(qi, b0_ref)
    o_ref[...] = _diff_finish(acc_ref, l_ref, 0, lam_ref[...], go_ref[...], out_scale)


def diff_prompt(q, k, v, bias_diag, bias_prev, lam, g_o, out_scale, nb, seq, t):
    dv = 2 * HEAD_DIM
    heads = q.shape[1] // dv
    nq = seq // t
    kv = pl.BlockSpec((seq, dv), lambda b, h, i: (b, h))
    bias = pl.BlockSpec((None, t, t), lambda b, h, i: (h, 0, 0))
    return pl.pallas_call(
        functools.partial(_diff_prompt_kernel, t=t, out_scale=out_scale),
        grid=(nb, heads, nq),
        in_specs=[pl.BlockSpec((t, dv), lambda b, h, i: (b * nq + i, h)), kv, kv, bias, bias,
                  pl.BlockSpec((1, 1), lambda b, h, i: (0, 0)),
                  pl.BlockSpec((1, dv), lambda b, h, i: (0, 0))],
        out_specs=pl.BlockSpec((t, dv), lambda b, h, i: (b * nq + i, h)),
        out_shape=jax.ShapeDtypeStruct((nb * seq, heads * dv), BF16),
        scratch_shapes=[pltpu.VMEM((2, t, 1), F32), pltpu.VMEM((2, t, 1), F32),
                        pltpu.VMEM((2, t, dv), F32)],
        compiler_params=_params("parallel", "parallel", "arbitrary"),
        name="diff_prompt",
    )(q, k, v, bias_diag, bias_prev, lam, g_o.reshape(1, dv))


def _diff_sample_kernel(q_ref, kc_ref, vc_ref, bc_ref, kn_ref, vn_ref, bn_ref, lam_ref, go_ref,
                        o_ref, m_ref, l_ref, acc_ref, *, heads, out_scale):
    kt = pl.program_id(1)
    last = pl.num_programs(1) - 1
    dv = 2 * HEAD_DIM

    @pl.when(kt == 0)
    def _():
        _softmax_init(m_ref, l_ref, acc_ref)

    for h in range(heads):
        v = vc_ref[:, h * dv:(h + 1) * dv].astype(BF16)
        for c in range(2):
            cols = slice((2 * h + c) * HEAD_DIM, (2 * h + c + 1) * HEAD_DIM)
            s = _nt_dot(q_ref[:, cols], kc_ref[:, cols].astype(BF16)) + bc_ref[h]
            _softmax_step(s, v, m_ref, l_ref, acc_ref, 2 * h + c)

    @pl.when(kt == last)
    def _():
        for h in range(heads):
            v = vn_ref[:, h * dv:(h + 1) * dv]
            for c in range(2):
                cols = slice((2 * h + c) * HEAD_DIM, (2 * h + c + 1) * HEAD_DIM)
                s = _nt_dot(q_ref[:, cols], kn_ref[:, cols]) + bn_ref[h]
                _softmax_step(s, v, m_ref, l_ref, acc_ref, 2 * h + c)
            o_ref[:, h * dv:(h + 1) * dv] = _diff_finish(
                acc_ref, l_ref, h, lam_ref[...], go_ref[...], out_scale)


def diff_sample(q, k, v, cache_k, cache_v, layer, bias_cache, bias_new, lam, g_o, out_scale,
                row0, nb, tq, tk):
    w = q.shape[1]
    dv = 2 * HEAD_DIM
    heads = w // dv
    past = cache_k.shape[2]
    nkt = past // tk
    blk0 = row0 // tq
    new = pl.BlockSpec((tq, w), lambda b, j: (blk0 + b, 0))
    cache = pl.BlockSpec((None, None, tk, w), lambda b, j: (layer, b, j, 0))
    return pl.pallas_call(
        functools.partial(_diff_sample_kernel, heads=heads, out_scale=out_scale),
        grid=(nb, nkt),
        in_specs=[new, cache, cache,
                  pl.BlockSpec((heads, tq, tk), lambda b, j: (0, 0, j)),
                  new, new,
                  pl.BlockSpec((heads, tq, tq), lambda b, j: (0, 0, 0)),
                  pl.BlockSpec((1, 1), lambda b, j: (0, 0)),
                  pl.BlockSpec((1, dv), lambda b, j: (0, 0))],
        out_specs=pl.BlockSpec((tq, w), lambda b, j: (b, 0)),
        out_shape=jax.ShapeDtypeStruct((nb * tq, w), BF16),
        scratch_shapes=[pltpu.VMEM((2 * heads, tq, 1), F32), pltpu.VMEM((2 * heads, tq, 1), F32),
                        pltpu.VMEM((2 * heads, tq, dv), F32)],
        compiler_params=_params("parallel", "arbitrary"),
        name="diff_sample",
    )(q, cache_k, cache_v, bias_cache, k, v, bias_new, lam, g_o.reshape(1, dv))


def _t5_bucket(rel):
    half = NUM_BUCKETS // 2
    max_exact = half // 2
    ret = jnp.where(rel > 0, half, 0)
    n = jnp.abs(rel)
    nf = jnp.maximum(n, 1).astype(F32)
    large = max_exact + (jnp.log(nf / max_exact) / math.log(MAX_DISTANCE / max_exact)
                         * (half - max_exact)).astype(jnp.int32)
    large = jnp.minimum(large, half - 1)
    return ret + jnp.where(n < max_exact, n, large)


def _rel_bias(table, q_pos, k_pos):
    bias = jnp.transpose(table[_t5_bucket(k_pos[None, :] - q_pos[:, None])], (2, 0, 1)).astype(F32)
    mask = (k_pos[None, :] // CHUNK) <= (q_pos[:, None] // CHUNK)
    return bias, mask


def _forward(cfg, x_prompt, x_sample, c_prompt, c_sample, cache_diff_k, cache_diff_v, cache_fox_k,
             cache_fox_v, cache_fox_logf, state_hgrn, rel_bias_table, hgrn_lb_logits, w_ada, b_ada,
             g_ffn1, w_ffn1_up, w_ffn1_down, g_mix, w_in, b_fox_f, g_hg_o, g_diff_q, g_diff_k,
             diff_lambda, g_diff_o, g_fox_q, g_fox_k, w_branch, w_out, g_ffn2, w_ffn2_up,
             w_ffn2_down):
    d, w = cfg.d_model, cfg.width
    nb_p, seq, nb_s, tq_s, past = cfg.batch, cfg.seq, cfg.dec_batch, cfg.dec_seq, cfg.past_len
    ntp, nt = cfg.nt_prompt, cfg.nt
    tm, tn, t = cfg.tm, cfg.tn, cfg.tq
    assert tq_s == SEG and seq % t == 0 and t >= MAX_DISTANCE and t % CHUNK == 0
    fox_h, diff_h, hg_h = cfg.fox_heads, cfg.diff_heads, cfg.hg_heads
    dff, dffp = cfg.d_ff, cfg.d_ff_pad

    n_seq = nb_p + nb_s
    n_seq_pad = ((n_seq + 7) // 8) * 8
    c_all = jnp.concatenate([c_prompt, c_sample, jnp.zeros((n_seq_pad - n_seq, d), F32)], axis=0)
    mod = ada_mods(c_all, w_ada, b_ada, min(cfg.t_ada, N_ADA * d))
    seg_to_seq = np.concatenate([np.repeat(np.arange(nb_p), seq // SEG), nb_p + np.arange(nb_s)])
    ns = seg_to_seq.shape[0]
    mod = mod[:, seg_to_seq, :].reshape(cfg.depth, ns, N_ADA, d)
    mod = jnp.transpose(mod, (0, 2, 1, 3))[:, :, :, None, :]

    x3 = jnp.concatenate([x_prompt.reshape(-1, SEG, d), x_sample.reshape(-1, SEG, d)], axis=0)

    cs = jnp.cumsum(jax.nn.softmax(hgrn_lb_logits.astype(F32), axis=0), axis=0)
    lbs = cs - cs[0:1]

    far = rel_bias_table[NUM_BUCKETS // 2 - 1].astype(F32)[:, None, None]
    pos_t = jnp.arange(t, dtype=jnp.int32)
    bias_d, mask_d = _rel_bias(rel_bias_table, pos_t + t, pos_t + t)
    bias_diag = jnp.where(mask_d[None], bias_d - far, NEG)
    bias_prev = _rel_bias(rel_bias_table, pos_t + t, pos_t)[0] - far
    q_pos_s = past + jnp.arange(tq_s, dtype=jnp.int32)
    bias_s, mask_s = _rel_bias(rel_bias_table, q_pos_s, jnp.arange(past + tq_s, dtype=jnp.int32))
    bias_s = jnp.where(mask_s[None], bias_s, NEG)
    bias_s_cache, bias_s_new = bias_s[:, :, :past], bias_s[:, :, past:]

    cache_dk = cache_diff_k.reshape(cfg.depth, nb_s, past, w)
    cache_dv = cache_diff_v.reshape(cfg.depth, nb_s, past, w)
    cache_fk = cache_fox_k.reshape(cfg.depth, nb_s, past, w)
    cache_fv = cache_fox_v.reshape(cfg.depth, nb_s, past, w)
    zero_state = jnp.zeros((nb_p, hg_h, HEAD_DIM, HEAD_DIM), F32)

    outs = [[] for _ in range(12)]
    for li in range(cfg.depth):
        ml = mod[li]
        def up_parts(w_up):
            pad = ((0, 0), (0, dffp - dff))
            return (jnp.pad(w_up[:, :dff], pad).astype(BF16), jnp.pad(w_up[:, dff:], pad).astype(BF16))

        def down_part(w_down):
            return jnp.pad(w_down, ((0, dffp - dff), (0, 0))).astype(BF16)

        wa1, wb1 = up_parts(w_ffn1_up[li])
        wd1 = down_part(w_ffn1_down[li])
        wa2, wb2 = up_parts(w_ffn2_up[li])
        wd2 = down_part(w_ffn2_down[li])
        w_main = w_in[li][:, :10 * w].astype(BF16)
        w_ff = jnp.pad(w_in[li][:, 10 * w:10 * w + fox_h], ((0, 0), (0, HEAD_DIM - fox_h))).astype(BF16)
        b_ff = jnp.pad(b_fox_f[li].astype(F32), (0, HEAD_DIM - fox_h)).reshape(1, HEAD_DIM)
        w_gates = w_in[li][:, 10 * w + fox_h:].astype(BF16)
        w_br = w_branch[li].astype(BF16)
        w_o = w_out[li].astype(BF16)

        h = norm_mod(x3, g_ffn1[li], ml, 1, 0, tm)
        x3 = mm_residual(ffn_up(h, wa1, wb1, tm, tn), wd1, x3, ml, 2, 0.5, tm, tn)

        h = norm_mod(x3, g_mix[li], ml, 4, 3, tm)
        lb = lbs[li].reshape(1, w)
        q_hg, g_hg, k_hg, v_hg, og = proj_hgrn(h, w_main, jnp.log(lb), jnp.log1p(-lb), 1.0 - lb, tm)
        dq, dk_f, dk_b, dv_f, dv_b = proj_attn(h, w_main, 4, g_diff_q[li], g_diff_k[li], tm)
        fq, fk_f, fk_b, fv_f, fv_b, logf = proj_attn(h, w_main, 7, g_fox_q[li], g_fox_k[li], tm,
                                                     w_ff, b_ff)
        gates = proj_gates(h, w_gates, tm, tn)
        logf = logf[:, :fox_h]

        o_hg_p, st_p = hgrn_mixer(q_hg, k_hg, v_hg, g_hg, og, zero_state, g_hg_o[li], 0, nb_p, seq)
        o_hg_s, st_s = hgrn_mixer(q_hg, k_hg, v_hg, g_hg, og, state_hgrn[li].astype(F32),
                                  g_hg_o[li], ntp, nb_s, tq_s)

        lam_init = 0.8 - 0.6 * math.exp(-0.3 * li)
        dl = diff_lambda[li].astype(F32)
        lam = (jnp.exp(jnp.sum(dl[0] * dl[1])) - jnp.exp(jnp.sum(dl[2] * dl[3])) + lam_init).reshape(1, 1)
        o_df_p = diff_prompt(dq, dk_b, dv_b, bias_diag, bias_prev, lam, g_diff_o[li],
                             1.0 - lam_init, nb_p, seq, t)
        o_df_s = diff_sample(dq, dk_b, dv_b, cache_dk, cache_dv, li, bias_s_cache, bias_s_new, lam,
                             g_diff_o[li], 1.0 - lam_init, ntp, nb_s, tq_s, min(cfg.tkc, past))

        logf_p = logf[:ntp].reshape(nb_p, seq, fox_h)
        logf_s = logf[ntp:].reshape(nb_s, tq_s, fox_h)
        ck_p = jnp.transpose(jnp.cumsum(logf_p, axis=1), (0, 2, 1)).reshape(nb_p, fox_h, seq // t, 1, t)
        cum_s = jnp.cumsum(jnp.concatenate([cache_fox_logf[li].astype(F32), logf_s], axis=1), axis=1)
        cum_s = jnp.transpose(cum_s, (0, 2, 1))
        o_fx_p = fox_prompt(fq, fk_b, fv_b, ck_p, nb_p, seq, t)
        o_fx_s = fox_sample(fq, fk_b, fv_b, cache_fk, cache_fv, li, cum_s[:, :, :past],
                            cum_s[:, :, past:], ntp, nb_s, tq_s, min(cfg.tkc, past))

        branches = [jnp.concatenate([a, b], axis=0) for a, b in
                    ((o_hg_p, o_hg_s), (o_df_p, o_df_s), (o_fx_p, o_fx_s))]
        merged = merge_branches(branches, gates, w_br, tm, tn)
        x3 = mm_residual(merged, w_o, x3, ml, 5, 1.0, tm, tn)

        h = norm_mod(x3, g_ffn2[li], ml, 7, 6, tm)
        x3 = mm_residual(ffn_up(h, wa2, wb2, tm, tn), wd2, x3, ml, 8, 0.5, tm, tn)

        for idx, (a, shape_p, shape_s) in enumerate((
                (dk_f, (nb_p, seq, diff_h, 2, HEAD_DIM), (nb_s, tq_s, diff_h, 2, HEAD_DIM)),
                (dv_f, (nb_p, seq, diff_h, 2 * HEAD_DIM), (nb_s, tq_s, diff_h, 2 * HEAD_DIM)),
                (fk_f, (nb_p, seq, fox_h, HEAD_DIM), (nb_s, tq_s, fox_h, HEAD_DIM)),
                (fv_f, (nb_p, seq, fox_h, HEAD_DIM), (nb_s, tq_s, fox_h, HEAD_DIM)),
                (logf, (nb_p, seq, fox_h), (nb_s, tq_s, fox_h)))):
            outs[idx].append(a[:ntp].reshape(shape_p))
            outs[6 + idx].append(a[ntp:].reshape(shape_s))
        outs[5].append(st_p)
        outs[11].append(st_s)

    y_prompt = x3[:ntp // SEG].reshape(nb_p, seq, d)
    y_sample = x3[ntp // SEG:].reshape(nb_s, tq_s, d)
    return (y_prompt, y_sample) + tuple(jnp.stack(o) for o in outs)


def kernel(x_prompt, x_sample, c_prompt, c_sample, cache_diff_k, cache_diff_v, cache_fox_k, cache_fox_v, cache_fox_logf, state_hgrn, rel_bias_table, hgrn_lb_logits, w_ada, b_ada, g_ffn1, w_ffn1_up, w_ffn1_down, g_mix, w_in, b_fox_f, g_hg_o, g_diff_q, g_diff_k, diff_lambda, g_diff_o, g_fox_q, g_fox_k, w_branch, w_out, g_ffn2, w_ffn2_up, w_ffn2_down):
    return _forward(FULL_CFG, x_prompt, x_sample, c_prompt, c_sample, cache_diff_k, cache_diff_v,
                    cache_fox_k, cache_fox_v, cache_fox_logf, state_hgrn, rel_bias_table,
                    hgrn_lb_logits, w_ada, b_ada, g_ffn1, w_ffn1_up, w_ffn1_down, g_mix, w_in,
                    b_fox_f, g_hg_o, g_diff_q, g_diff_k, diff_lambda, g_diff_o, g_fox_q, g_fox_k,
                    w_branch, w_out, g_ffn2, w_ffn2_up, w_ffn2_down)
```

```python
import functools
import math
from typing import NamedTuple

import jax
import jax.numpy as jnp
import numpy as np
from jax import lax
from jax.experimental import pallas as pl
from jax.experimental.pallas import tpu as pltpu

F32 = jnp.float32
BF16 = jnp.bfloat16

HEAD_DIM = 128
SEG = 64
CHUNK = 64
NUM_BUCKETS = 32
MAX_DISTANCE = 128
N_ADA = 9
N_BRANCH = 3
EPS = 1e-6
NEG = -1e30
HG_CHUNK = 128
VMEM_LIMIT_BYTES = 56 * 1024 * 1024


class Cfg(NamedTuple):
    d_model: int
    batch: int
    seq: int
    depth: int
    dec_batch: int
    dec_seq: int
    past_len: int
    tm: int = 512
    tm_big: int = 1024
    tn: int = 512
    tq: int = 512
    fox_group: int = 2
    diff_group: int = 2
    t_ada: int = 1024

    @property
    def width(self):
        return self.d_model // 2

    @property
    def hg_heads(self):
        return self.width // HEAD_DIM

    @property
    def diff_heads(self):
        return self.width // (2 * HEAD_DIM)

    @property
    def fox_heads(self):
        return self.width // HEAD_DIM

    @property
    def d_ff(self):
        return ((8 * self.d_model // 3 + 127) // 128) * 128

    @property
    def d_ff_pad(self):
        return ((self.d_ff + self.tn - 1) // self.tn) * self.tn

    @property
    def nt_prompt(self):
        return self.batch * self.seq

    @property
    def nt_sample(self):
        return self.dec_batch * self.dec_seq

    @property
    def nt(self):
        return self.nt_prompt + self.nt_sample


FULL_CFG = Cfg(d_model=2048, batch=2, seq=4096, depth=4, dec_batch=32, dec_seq=64, past_len=2048)


def _params(*sem):
    return pltpu.CompilerParams(dimension_semantics=sem, vmem_limit_bytes=VMEM_LIMIT_BYTES)


def _silu(x):
    return x * jax.nn.sigmoid(x)


def _log_sigmoid(x):
    return jnp.minimum(x, 0.0) - jnp.log1p(jnp.exp(-jnp.abs(x)))


def _nt_dot(a, b):
    return lax.dot_general(a, b, (((1,), (1,)), ((), ())), preferred_element_type=F32)


def _tn_dot(a, b):
    return lax.dot_general(a, b, (((0,), (0,)), ((), ())), preferred_element_type=F32)


def _split3(x):
    x1 = x.astype(BF16)
    r1 = x - x1.astype(F32)
    x2 = r1.astype(BF16)
    x3 = (r1 - x2.astype(F32)).astype(BF16)
    return x1, x2, x3


def _exact_left_dot(m, x):
    return sum(jnp.dot(m, p, preferred_element_type=F32) for p in _split3(x))


def _ada_kernel(c_ref, w_ref, b_ref, o_ref):
    a = _silu(c_ref[...]).astype(BF16)
    o_ref[...] = jnp.dot(a, w_ref[...].astype(BF16), preferred_element_type=F32) + b_ref[...]


def ada_mods(c_all, w_ada, b_ada, tn):
    depth, d, n = w_ada.shape
    r = c_all.shape[0]
    return pl.pallas_call(
        _ada_kernel,
        grid=(depth, n // tn),
        in_specs=[pl.BlockSpec((r, d), lambda l, j: (0, 0)),
                  pl.BlockSpec((None, d, tn), lambda l, j: (l, 0, j)),
                  pl.BlockSpec((None, 1, tn), lambda l, j: (l, 0, j))],
        out_specs=pl.BlockSpec((None, r, tn), lambda l, j: (l, 0, j)),
        out_shape=jax.ShapeDtypeStruct((depth, r, n), F32),
        compiler_params=_params("parallel", "parallel"),
        name="ada_mods",
    )(c_all, w_ada, b_ada.reshape(depth, 1, n))


def _norm_mod_kernel(x_ref, g_ref, sc_ref, sh_ref, o_ref):
    x = x_ref[...]
    ms = jnp.mean(x * x, axis=-1, keepdims=True)
    y = x * lax.rsqrt(ms + EPS) * g_ref[...]
    y = y * (1.0 + sc_ref[...]) + sh_ref[...]
    o_ref[...] = y.reshape(o_ref.shape).astype(BF16)


def norm_mod(x3, g, mods, i_scale, i_shift, tm):
    ns, seg, d = x3.shape
    s = tm // seg
    return pl.pallas_call(
        _norm_mod_kernel,
        grid=(ns // s,),
        in_specs=[pl.BlockSpec((s, seg, d), lambda i: (i, 0, 0)),
                  pl.BlockSpec((1, d), lambda i: (0, 0)),
                  pl.BlockSpec((None, s, 1, d), lambda i: (i_scale, i, 0, 0)),
                  pl.BlockSpec((None, s, 1, d), lambda i: (i_shift, i, 0, 0))],
        out_specs=pl.BlockSpec((tm, d), lambda i: (i, 0)),
        out_shape=jax.ShapeDtypeStruct((ns * seg, d), BF16),
        compiler_params=_params("parallel"),
        name="norm_mod",
    )(x3, g.reshape(1, d), mods, mods)


def _ffn_up_kernel(h_ref, wa_ref, wb_ref, o_ref):
    h = h_ref[...]
    a = jnp.dot(h, wa_ref[...], preferred_element_type=F32)
    b = jnp.dot(h, wb_ref[...], preferred_element_type=F32)
    o_ref[...] = (_silu(a) * b).astype(BF16)


def ffn_up(h, wa, wb, tm, tn):
    m, k = h.shape
    f = wa.shape[1]
    return pl.pallas_call(
        _ffn_up_kernel,
        grid=(m // tm, f // tn),
        in_specs=[pl.BlockSpec((tm, k), lambda i, j: (i, 0)),
                  pl.BlockSpec((k, tn), lambda i, j: (0, j)),
                  pl.BlockSpec((k, tn), lambda i, j: (0, j))],
        out_specs=pl.BlockSpec((tm, tn), lambda i, j: (i, j)),
        out_shape=jax.ShapeDtypeStruct((m, f), BF16),
        compiler_params=_params("parallel", "arbitrary"),
        name="ffn_up",
    )(h, wa, wb)


def _mm_residual_kernel(a_ref, w_ref, x_ref, gt_ref, o_ref, *, coef):
    acc = jnp.dot(a_ref[...], w_ref[...], preferred_element_type=F32)
    o_ref[...] = x_ref[...] + (coef * gt_ref[...]) * acc.reshape(x_ref.shape)


def mm_residual(a, w, x3, mods, i_gate, coef, tm, tn):
    m, k = a.shape
    n = w.shape[1]
    ns, seg, _ = x3.shape
    s = tm // seg
    return pl.pallas_call(
        functools.partial(_mm_residual_kernel, coef=coef),
        grid=(m // tm, n // tn),
        in_specs=[pl.BlockSpec((tm, k), lambda i, j: (i, 0)),
                  pl.BlockSpec((k, tn), lambda i, j: (0, j)),
                  pl.BlockSpec((s, seg, tn), lambda i, j: (i, 0, j)),
                  pl.BlockSpec((None, s, 1, tn), lambda i, j: (i_gate, i, 0, j))],
        out_specs=pl.BlockSpec((s, seg, tn), lambda i, j: (i, 0, j)),
        out_shape=jax.ShapeDtypeStruct(x3.shape, F32),
        compiler_params=_params("parallel", "arbitrary"),
        name="mm_residual",
    )(a, w, x3, mods)


def _proj_hgrn_kernel(h_ref, w_ref, la_ref, l1m_ref, oml_ref, q_ref, g_ref, k_ref, v_ref, og_ref):
    j = pl.program_id(1)
    acc = jnp.dot(h_ref[...], w_ref[...], preferred_element_type=F32)

    @pl.when(j == 0)
    def _():
        q_ref[...] = (_silu(acc) * (HEAD_DIM ** -0.5)).astype(BF16)

    @pl.when(j == 1)
    def _():
        la = la_ref[...]
        y = l1m_ref[...] + _log_sigmoid(acc)
        g_ref[...] = jnp.maximum(la, y) + jnp.log1p(jnp.exp(-jnp.abs(la - y)))
        k_ref[...] = (oml_ref[...] * jax.nn.sigmoid(-acc)).astype(BF16)

    @pl.when(j == 2)
    def _():
        v_ref[...] = acc.astype(BF16)

    @pl.when(j == 3)
    def _():
        og_ref[...] = jax.nn.sigmoid(acc).astype(BF16)


def proj_hgrn(h, w_main, log_lb, log1m_lb, one_m_lb, tm):
    m, k = h.shape
    w = log_lb.shape[-1]
    row = lambda i, j: (i, 0)
    vec = pl.BlockSpec((1, w), lambda i, j: (0, 0))
    return pl.pallas_call(
        _proj_hgrn_kernel,
        grid=(m // tm, 4),
        in_specs=[pl.BlockSpec((tm, k), row),
                  pl.BlockSpec((k, w), lambda i, j: (0, j)),
                  vec, vec, vec],
        out_specs=[pl.BlockSpec((tm, w), row)] * 5,
        out_shape=[jax.ShapeDtypeStruct((m, w), BF16),
                   jax.ShapeDtypeStruct((m, w), F32),
                   jax.ShapeDtypeStruct((m, w), BF16),
                   jax.ShapeDtypeStruct((m, w), BF16),
                   jax.ShapeDtypeStruct((m, w), BF16)],
        compiler_params=_params("parallel", "arbitrary"),
        name="proj_hgrn",
    )(h, w_main, log_lb, log1m_lb, one_m_lb)


def _head_rms(acc, g):
    outs = []
    for c in range(acc.shape[1] // HEAD_DIM):
        ch = acc[:, c * HEAD_DIM:(c + 1) * HEAD_DIM]
        ms = jnp.mean(ch * ch, axis=-1, keepdims=True)
        outs.append(ch * lax.rsqrt(ms + EPS) * g)
    return outs


def _proj_attn_kernel(*refs, q_scale, with_forget):
    if with_forget:
        (h_ref, w_ref, gq_ref, gk_ref, wf_ref, bf_ref,
         qb_ref, kf_ref, kb_ref, vf_ref, vb_ref, lf_ref) = refs
    else:
        h_ref, w_ref, gq_ref, gk_ref, qb_ref, kf_ref, kb_ref, vf_ref, vb_ref = refs
    j = pl.program_id(1)
    acc = jnp.dot(h_ref[...], w_ref[...], preferred_element_type=F32)

    @pl.when(j == 0)
    def _():
        for c, ch in enumerate(_head_rms(acc, gq_ref[...])):
            qb_ref[:, c * HEAD_DIM:(c + 1) * HEAD_DIM] = (ch * q_scale).astype(BF16)
        if with_forget:
            f = jnp.dot(h_ref[...], wf_ref[...], preferred_element_type=F32) + bf_ref[...]
            lf_ref[...] = _log_sigmoid(f)

    @pl.when(j == 1)
    def _():
        for c, ch in enumerate(_head_rms(acc, gk_ref[...])):
            kf_ref[:, c * HEAD_DIM:(c + 1) * HEAD_DIM] = ch
            kb_ref[:, c * HEAD_DIM:(c + 1) * HEAD_DIM] = ch.astype(BF16)

    @pl.when(j == 2)
    def _():
        vf_ref[...] = acc
        vb_ref[...] = acc.astype(BF16)


def proj_attn(h, w_main, col_block, g_q, g_k, tm, w_forget=None, b_forget=None):
    m, k = h.shape
    w = w_main.shape[1] // 10
    with_forget = w_forget is not None
    row = lambda i, j: (i, 0)
    vec = pl.BlockSpec((1, HEAD_DIM), lambda i, j: (0, 0))
    in_specs = [pl.BlockSpec((tm, k), row),
                pl.BlockSpec((k, w), lambda i, j: (0, col_block + j)),
                vec, vec]
    args = [h, w_main, g_q.reshape(1, HEAD_DIM), g_k.reshape(1, HEAD_DIM)]
    out_specs = [pl.BlockSpec((tm, w), row)] * 5
    out_shape = [jax.ShapeDtypeStruct((m, w), BF16),
                 jax.ShapeDtypeStruct((m, w), F32), jax.ShapeDtypeStruct((m, w), BF16),
                 jax.ShapeDtypeStruct((m, w), F32), jax.ShapeDtypeStruct((m, w), BF16)]
    if with_forget:
        in_specs += [pl.BlockSpec((k, HEAD_DIM), lambda i, j: (0, 0)), vec]
        args += [w_forget, b_forget]
        out_specs.append(pl.BlockSpec((tm, HEAD_DIM), row))
        out_shape.append(jax.ShapeDtypeStruct((m, HEAD_DIM), F32))
    return pl.pallas_call(
        functools.partial(_proj_attn_kernel, q_scale=HEAD_DIM ** -0.5, with_forget=with_forget),
        grid=(m // tm, 3),
        in_specs=in_specs, out_specs=out_specs, out_shape=out_shape,
        compiler_params=_params("parallel", "arbitrary"),
        name="proj_fox" if with_forget else "proj_diff",
    )(*args)


def _proj_gates_kernel(h_ref, w_ref, o_ref):
    acc = jnp.dot(h_ref[...], w_ref[...], preferred_element_type=F32)
    o_ref[...] = jax.nn.sigmoid(acc).astype(BF16)


def proj_gates(h, w_gates, tm, tn):
    m, k = h.shape
    n = w_gates.shape[1]
    return pl.pallas_call(
        _proj_gates_kernel,
        grid=(m // tm, n // tn),
        in_specs=[pl.BlockSpec((tm, k), lambda i, j: (i, 0)),
                  pl.BlockSpec((k, tn), lambda i, j: (0, j))],
        out_specs=pl.BlockSpec((tm, tn), lambda i, j: (i, j)),
        out_shape=jax.ShapeDtypeStruct((m, n), BF16),
        compiler_params=_params("parallel", "arbitrary"),
        name="proj_gates",
    )(h, w_gates)


def _merge_kernel(b0_ref, b1_ref, b2_ref, g0_ref, g1_ref, g2_ref, w_ref, o_ref):
    acc = None
    for n, (b_ref, g_ref) in enumerate(((b0_ref, g0_ref), (b1_ref, g1_ref), (b2_ref, g2_ref))):
        up = jnp.dot(b_ref[...], w_ref[n], preferred_element_type=F32)
        t = g_ref[...].astype(F32) * up
        acc = t if acc is None else acc + t
    o_ref[...] = acc.astype(BF16)


def merge_branches(branches, gates, w_branch, tm, tn):
    m, w = branches[0].shape
    d = w_branch.shape[2]
    nj = d // tn
    bspec = pl.BlockSpec((tm, w), lambda i, j: (i, 0))
    gspecs = [pl.BlockSpec((tm, tn), functools.partial(lambda i, j, n: (i, n * nj + j), n=n))
              for n in range(N_BRANCH)]
    return pl.pallas_call(
        _merge_kernel,
        grid=(m // tm, nj),
        in_specs=[bspec, bspec, bspec] + gspecs +
                 [pl.BlockSpec((N_BRANCH, w, tn), lambda i, j: (0, 0, j))],
        out_specs=pl.BlockSpec((tm, tn), lambda i, j: (i, j)),
        out_shape=jax.ShapeDtypeStruct((m, d), BF16),
        compiler_params=_params("parallel", "arbitrary"),
        name="merge_branches",
    )(*branches, gates, gates, gates, w_branch)


def _hgrn_level_tables(c):
    t = np.arange(c)[:, None]
    u = np.arange(c)[None, :]
    sums = [(u <= t)]
    masks = [(t == u)]
    w = 1
    while w < c:
        upper = (t // w) % 2 == 1
        q_sum = upper & (u >= (t // w) * w) & (u <= t)
        k_sum = (~upper) & (u > t) & (u <= (t // w) * w + w - 1)
        sums.append(q_sum | k_sum)
        masks.append((t // (2 * w) == u // (2 * w)) & upper & ((u // w) % 2 == 0))
        w *= 2
    return (np.stack(sums).astype(np.float32).reshape(-1, c), np.stack(masks).astype(np.float32))


def _hgrn_kernel(q_ref, k_ref, v_ref, g_ref, og_ref, s0_ref, gn_ref, sums_ref, masks_ref,
                 o_ref, sout_ref, st_ref, *, heads, rows):
    c = pl.program_id(1)
    last = pl.num_programs(1) - 1
    C = HG_CHUNK
    levels = masks_ref.shape[0] - 1

    @pl.when(c == 0)
    def _():
        for h in range(heads):
            st_ref[h] = s0_ref[h].T

    gn = gn_ref[...]

    def pad_rows(x):
        if rows == C:
            return x
        return jnp.concatenate([x, jnp.zeros((C - rows, x.shape[1]), x.dtype)], axis=0)

    qb = pad_rows(q_ref[...])
    kb = pad_rows(k_ref[...])
    v = pad_rows(v_ref[...])
    q = qb.astype(F32)
    k = kb.astype(F32)
    sums = _exact_left_dot(sums_ref[...], pad_rows(g_ref[...]))
    b = sums[:C]
    qe = (q * jnp.exp(b)).astype(BF16)
    b_last = b[C - 1:C, :]
    kd = (k * jnp.exp(b_last - b)).astype(BF16)
    decay = jnp.exp(b_last)
    cols = [slice(h * HEAD_DIM, (h + 1) * HEAD_DIM) for h in range(heads)]
    att = [masks_ref[0] * _nt_dot(qb[:, s], kb[:, s]) for s in cols]
    for lv in range(1, levels + 1):
        e = jnp.exp(sums[lv * C:(lv + 1) * C])
        qw = (q * e).astype(BF16)
        kw = (k * e).astype(BF16)
        mask = masks_ref[lv]
        for h, s in enumerate(cols):
            att[h] = att[h] + mask * _nt_dot(qw[:, s], kw[:, s])
    for h, s in enumerate(cols):
        st = st_ref[h]
        o = (_nt_dot(qe[:, s], st.astype(BF16))
             + jnp.dot(att[h].astype(BF16), v[:, s], preferred_element_type=F32))
        st_ref[h] = st * decay[:, s] + _tn_dot(v[:, s], kd[:, s])
        o = o[:rows]
        ms = jnp.mean(o * o, axis=-1, keepdims=True)
        o = o * lax.rsqrt(ms + EPS) * gn * og_ref[:, s].astype(F32)
        o_ref[:, s] = o.astype(BF16)

    @pl.when(c == last)
    def _():
        for h in range(heads):
            sout_ref[h] = st_ref[h].T


def hgrn_mixer(q, k, v, g, og, s0, g_norm, row0, nb, t):
    w = q.shape[1]
    heads = w // HEAD_DIM
    rows = min(t, HG_CHUNK)
    nc = t // rows
    blk0 = row0 // rows
    sums, masks = _hgrn_level_tables(HG_CHUNK)
    tok = pl.BlockSpec((rows, w), lambda b, c: (blk0 + b * nc + c, 0))
    state = pl.BlockSpec((None, heads, HEAD_DIM, HEAD_DIM), lambda b, c: (b, 0, 0, 0))
    return pl.pallas_call(
        functools.partial(_hgrn_kernel, heads=heads, rows=rows),
        grid=(nb, nc),
        in_specs=[tok, tok, tok, tok, tok, state, pl.BlockSpec((1, HEAD_DIM), lambda b, c: (0, 0)),
                  pl.BlockSpec(sums.shape, lambda b, c: (0, 0)),
                  pl.BlockSpec(masks.shape, lambda b, c: (0, 0, 0))],
        out_specs=[pl.BlockSpec((rows, w), lambda b, c: (b * nc + c, 0)), state],
        out_shape=[jax.ShapeDtypeStruct((nb * t, w), BF16),
                   jax.ShapeDtypeStruct((nb, heads, HEAD_DIM, HEAD_DIM), F32)],
        scratch_shapes=[pltpu.VMEM((heads, HEAD_DIM, HEAD_DIM), F32)],
        compiler_params=_params("parallel", "arbitrary"),
        name="hgrn_mixer",
    )(q, k, v, g, og, s0, g_norm.reshape(1, HEAD_DIM), jnp.asarray(sums, BF16), jnp.asarray(masks))


def _cumsum_kernel(x_ref, o_ref):
    rows, n, lanes = x_ref.shape
    ki = lax.broadcasted_iota(jnp.int32, (lanes, lanes), 0)
    ji = lax.broadcasted_iota(jnp.int32, (lanes, lanes), 1)
    upper = (ki <= ji).astype(BF16)
    lower = (ji < ki).astype(BF16)
    ones = jnp.ones((lanes, lanes), BF16)
    for r in range(rows):
        parts = _split3(x_ref[r])
        within = sum(jnp.dot(p, upper, preferred_element_type=F32) for p in parts)
        total = sum(jnp.dot(p, ones, preferred_element_type=F32) for p in parts)
        total = jnp.concatenate([total, jnp.zeros((lanes - n, lanes), F32)], axis=0)
        o_ref[r] = within + _exact_left_dot(lower, total)[:n]


def cumsum_time(x):
    b, t, h = x.shape
    lanes = HEAD_DIM
    tp = -(-t // (8 * lanes)) * (8 * lanes)
    n = tp // lanes
    assert n <= lanes
    xt = jnp.pad(jnp.transpose(x, (0, 2, 1)), ((0, 0), (0, 0), (0, tp - t))).reshape(b * h, n, lanes)
    rows = 8 if (b * h) % 8 == 0 else b * h
    y = pl.pallas_call(
        _cumsum_kernel,
        grid=(b * h // rows,),
        in_specs=[pl.BlockSpec((rows, n, lanes), lambda i: (i, 0, 0))],
        out_specs=pl.BlockSpec((rows, n, lanes), lambda i: (i, 0, 0)),
        out_shape=jax.ShapeDtypeStruct((b * h, n, lanes), F32),
        compiler_params=_params("parallel"),
        name="cumsum_time",
    )(xt)
    return y.reshape(b, h, tp)[:, :, :t]


def _softmax_init(m_ref, l_ref, acc_ref):
    m_ref[...] = jnp.full(m_ref.shape, NEG, F32)
    l_ref[...] = jnp.zeros(l_ref.shape, F32)
    acc_ref[...] = jnp.zeros(acc_ref.shape, F32)


def _softmax_step_t(st, vt, m_ref, l_ref, acc_ref, idx):
    m_old = m_ref[idx]
    m_new = jnp.maximum(m_old, jnp.max(st, axis=0, keepdims=True))
    alpha = jnp.exp(m_old - m_new)
    p = jnp.exp(st - m_new)
    l_ref[idx] = alpha * l_ref[idx] + jnp.sum(p, axis=0, keepdims=True)
    acc_ref[idx] = alpha * acc_ref[idx] + jnp.dot(vt, p.astype(BF16), preferred_element_type=F32)
    m_ref[idx] = m_new


def _fox_prompt_kernel(q_ref, k_ref, vt_ref, ck_ref, o_ref, m_ref, l_ref, acc_ref, *, t, group):
    qi = pl.program_id(2)
    _softmax_init(m_ref, l_ref, acc_ref)
    reps = t // HEAD_DIM

    def tile(kt, mask):
        rows = pl.ds(pl.multiple_of(kt * t, t), t)
        for h in range(group):
            cols = slice(h * HEAD_DIM, (h + 1) * HEAD_DIM)
            ck = ck_ref[h, rows, :]
            st = _nt_dot(k_ref[rows, cols], q_ref[:, cols]) - jnp.concatenate([ck] * reps, axis=1)
            if mask is not None:
                st = jnp.where(mask, st, NEG)
            _softmax_step_t(st, vt_ref[cols, rows], m_ref, l_ref, acc_ref, h)

    def body(kt, carry):
        tile(kt, None)
        return carry

    lax.fori_loop(0, qi, body, 0)
    key = lax.broadcasted_iota(jnp.int32, (t, t), 0)
    query = lax.broadcasted_iota(jnp.int32, (t, t), 1)
    tile(qi, key <= query)
    for h in range(group):
        o_ref[:, h * HEAD_DIM:(h + 1) * HEAD_DIM] = (acc_ref[h] / l_ref[h]).T.astype(BF16)


def fox_prompt(q, k, vt, ck, nb, seq, t, group):
    heads = q.shape[1] // HEAD_DIM
    group = min(group, heads)
    gw = group * HEAD_DIM
    nq = seq // t
    return pl.pallas_call(
        functools.partial(_fox_prompt_kernel, t=t, group=group),
        grid=(nb, heads // group, nq),
        in_specs=[pl.BlockSpec((t, gw), lambda b, h, i: (b * nq + i, h)),
                  pl.BlockSpec((seq, gw), lambda b, h, i: (b, h)),
                  pl.BlockSpec((gw, seq), lambda b, h, i: (h, b)),
                  pl.BlockSpec((None, group, seq, HEAD_DIM), lambda b, h, i: (b, h, 0, 0))],
        out_specs=pl.BlockSpec((t, gw), lambda b, h, i: (b * nq + i, h)),
        out_shape=jax.ShapeDtypeStruct((nb * seq, heads * HEAD_DIM), BF16),
        scratch_shapes=[pltpu.VMEM((group, 1, t), F32), pltpu.VMEM((group, 1, t), F32),
                        pltpu.VMEM((group, HEAD_DIM, t), F32)],
        compiler_params=_params("parallel", "parallel", "arbitrary"),
        name="fox_prompt",
    )(q, k, vt, ck)


def _diff_prompt_kernel(q_ref, k_ref, vt_ref, b0_ref, b1_ref, lam_ref, go_ref, o_ref,
                        m_ref, l_ref, acc_ref, *, t, group, out_scale):
    qi = pl.program_id(2)
    dv = 2 * HEAD_DIM
    _softmax_init(m_ref, l_ref, acc_ref)

    def tile(kt, bias_ref):
        rows = pl.ds(pl.multiple_of(kt * t, t), t)
        for h in range(group):
            vt = vt_ref[h * dv:(h + 1) * dv, rows]
            for c in range(2):
                cols = slice((2 * h + c) * HEAD_DIM, (2 * h + c + 1) * HEAD_DIM)
                st = _nt_dot(k_ref[rows, cols], q_ref[:, cols])
                if bias_ref is not None:
                    st = st + bias_ref[h]
                _softmax_step_t(st, vt, m_ref, l_ref, acc_ref, 2 * h + c)

    def body(kt, carry):
        tile(kt, None)
        return carry

    lax.fori_loop(0, jnp.maximum(qi - 1, 0), body, 0)

    @pl.when(qi >= 1)
    def _():
        tile(qi - 1, b1_ref)

    tile(qi, b0_ref)
    lam = lam_ref[...]
    for h in range(group):
        ot = acc_ref[2 * h] / l_ref[2 * h] - lam * (acc_ref[2 * h + 1] / l_ref[2 * h + 1])
        ms = jnp.mean(ot * ot, axis=0, keepdims=True)
        ot = ot * lax.rsqrt(ms + EPS) * (go_ref[...] * out_scale)
        o_ref[:, h * dv:(h + 1) * dv] = ot.T.astype(BF16)


def diff_prompt(q, k, vt, bias_diag, bias_prev, lam, g_o, out_scale, nb, seq, t, group):
    dv = 2 * HEAD_DIM
    heads = q.shape[1] // dv
    group = min(group, heads)
    gw = group * dv
    nq = seq // t
    bias = pl.BlockSpec((group, t, t), lambda b, h, i: (h, 0, 0))
    return pl.pallas_call(
        functools.partial(_diff_prompt_kernel, t=t, group=group, out_scale=out_scale),
        grid=(nb, heads // group, nq),
        in_specs=[pl.BlockSpec((t, gw), lambda b, h, i: (b * nq + i, h)),
                  pl.BlockSpec((seq, gw), lambda b, h, i: (b, h)),
                  pl.BlockSpec((gw, seq), lambda b, h, i: (h, b)),
                  bias, bias,
                  pl.BlockSpec((1, 1), lambda b, h, i: (0, 0)),
                  pl.BlockSpec((dv, 1), lambda b, h, i: (0, 0))],
        out_specs=pl.BlockSpec((t, gw), lambda b, h, i: (b * nq + i, h)),
        out_shape=jax.ShapeDtypeStruct((nb * seq, heads * dv), BF16),
        scratch_shapes=[pltpu.VMEM((2 * group, 1, t), F32), pltpu.VMEM((2 * group, 1, t), F32),
                        pltpu.VMEM((2 * group, dv, t), F32)],
        compiler_params=_params("parallel", "parallel", "arbitrary"),
        name="diff_prompt",
    )(q, k, vt, bias_diag, bias_prev, lam, g_o.reshape(dv, 1))


def _two_block_softmax(s_cache, s_new, v_cache, v_new):
    m = jnp.maximum(jnp.max(s_cache, axis=-1, keepdims=True), jnp.max(s_new, axis=-1, keepdims=True))
    p_cache = jnp.exp(s_cache - m)
    p_new = jnp.exp(s_new - m)
    l = jnp.sum(p_cache, axis=-1, keepdims=True) + jnp.sum(p_new, axis=-1, keepdims=True)
    o = (jnp.dot(p_cache.astype(BF16), v_cache, preferred_element_type=F32)
         + jnp.dot(p_new.astype(BF16), v_new, preferred_element_type=F32))
    return o / l


def _fox_sample_kernel(q_ref, kc_ref, vc_ref, ckc_ref, kn_ref, vn_ref, ckn_ref, o_ref, *, heads):
    tq = q_ref.shape[0]
    past = kc_ref.shape[0] // heads
    ri = lax.broadcasted_iota(jnp.int32, (tq, tq), 0)
    ci = lax.broadcasted_iota(jnp.int32, (tq, tq), 1)
    for h in range(heads):
        cols = slice(h * HEAD_DIM, (h + 1) * HEAD_DIM)
        head_rows = pl.ds(h, past, stride=heads)
        q = q_ref[:, cols]
        s_cache = _nt_dot(q, kc_ref[head_rows, :].astype(BF16)) - ckc_ref[h:h + 1, :]
        s_new = jnp.where(ci <= ri, _nt_dot(q, kn_ref[:, cols]) - ckn_ref[h:h + 1, :], NEG)
        o = _two_block_softmax(s_cache, s_new, vc_ref[head_rows, :].astype(BF16), vn_ref[:, cols])
        o_ref[:, cols] = o.astype(BF16)


def fox_sample(q, k, v, cache_k, cache_v, layer, ck_cache, ck_new, row0, nb, tq):
    w = q.shape[1]
    heads = w // HEAD_DIM
    past = cache_k.shape[2] // heads
    blk0 = row0 // tq
    new = pl.BlockSpec((tq, w), lambda b: (blk0 + b, 0))
    cache = pl.BlockSpec((None, None, past * heads, HEAD_DIM), lambda b: (layer, b, 0, 0))
    return pl.pallas_call(
        functools.partial(_fox_sample_kernel, heads=heads),
        grid=(nb,),
        in_specs=[new, cache, cache,
                  pl.BlockSpec((None, heads, past), lambda b: (b, 0, 0)),
                  new, new,
                  pl.BlockSpec((None, heads, tq), lambda b: (b, 0, 0))],
        out_specs=pl.BlockSpec((tq, w), lambda b: (b, 0)),
        out_shape=jax.ShapeDtypeStruct((nb * tq, w), BF16),
        compiler_params=_params("parallel"),
        name="fox_sample",
    )(q, cache_k, cache_v, ck_cache, k, v, ck_new)


def _diff_sample_kernel(q_ref, kc_ref, vc_ref, bc_ref, kn_ref, vn_ref, bn_ref, lam_ref, go_ref,
                        o_ref, *, heads, out_scale):
    dv = 2 * HEAD_DIM
    past = vc_ref.shape[0] // (2 * heads)
    lam = lam_ref[...]
    for h in range(heads):
        halves = [vc_ref[pl.ds(half * heads + h, past, stride=2 * heads), :] for half in range(2)]
        v_cache = jnp.concatenate(halves, axis=1).astype(BF16)
        v_new = vn_ref[:, h * dv:(h + 1) * dv]
        maps = []
        for c in range(2):
            cols = slice((2 * h + c) * HEAD_DIM, (2 * h + c + 1) * HEAD_DIM)
            map_rows = pl.ds(2 * h + c, past, stride=2 * heads)
            q = q_ref[:, cols]
            s_cache = _nt_dot(q, kc_ref[map_rows, :].astype(BF16)) + bc_ref[h]
            s_new = _nt_dot(q, kn_ref[:, cols]) + bn_ref[h]
            maps.append(_two_block_softmax(s_cache, s_new, v_cache, v_new))
        o = maps[0] - lam * maps[1]
        ms = jnp.mean(o * o, axis=-1, keepdims=True)
        o_ref[:, h * dv:(h + 1) * dv] = (o * lax.rsqrt(ms + EPS) * (go_ref[...] * out_scale)).astype(BF16)


def diff_sample(q, k, v, cache_k, cache_v, layer, bias_cache, bias_new, lam, g_o, out_scale,
                row0, nb, tq):
    w = q.shape[1]
    dv = 2 * HEAD_DIM
    heads = w // dv
    past = cache_v.shape[2] // (2 * heads)
    blk0 = row0 // tq
    new = pl.BlockSpec((tq, w), lambda b: (blk0 + b, 0))
    cache = pl.BlockSpec((None, None, past * 2 * heads, HEAD_DIM), lambda b: (layer, b, 0, 0))
    return pl.pallas_call(
        functools.partial(_diff_sample_kernel, heads=heads, out_scale=out_scale),
        grid=(nb,),
        in_specs=[new, cache, cache,
                  pl.BlockSpec((heads, tq, past), lambda b: (0, 0, 0)),
                  new, new,
                  pl.BlockSpec((heads, tq, tq), lambda b: (0, 0, 0)),
                  pl.BlockSpec((1, 1), lambda b: (0, 0)),
                  pl.BlockSpec((1, dv), lambda b: (0, 0))],
        out_specs=pl.BlockSpec((tq, w), lambda b: (b, 0)),
        out_shape=jax.ShapeDtypeStruct((nb * tq, w), BF16),
        compiler_params=_params("parallel"),
        name="diff_sample",
    )(q, cache_k, cache_v, bias_cache, k, v, bias_new, lam, g_o.reshape(1, dv))


def _t5_bucket(rel):
    half = NUM_BUCKETS // 2
    max_exact = half // 2
    ret = jnp.where(rel > 0, half, 0)
    n = jnp.abs(rel)
    nf = jnp.maximum(n, 1).astype(F32)
    large = max_exact + (jnp.log(nf / max_exact) / math.log(MAX_DISTANCE / max_exact)
                         * (half - max_exact)).astype(jnp.int32)
    large = jnp.minimum(large, half - 1)
    return ret + jnp.where(n < max_exact, n, large)


def _rel_bias(table, q_pos, k_pos):
    bias = jnp.transpose(table[_t5_bucket(k_pos[None, :] - q_pos[:, None])], (2, 0, 1)).astype(F32)
    mask = (k_pos[None, :] // CHUNK) <= (q_pos[:, None] // CHUNK)
    return bias, mask


def _forward(cfg, x_prompt, x_sample, c_prompt, c_sample, cache_diff_k, cache_diff_v, cache_fox_k,
             cache_fox_v, cache_fox_logf, state_hgrn, rel_bias_table, hgrn_lb_logits, w_ada, b_ada,
             g_ffn1, w_ffn1_up, w_ffn1_down, g_mix, w_in, b_fox_f, g_hg_o, g_diff_q, g_diff_k,
             diff_lambda, g_diff_o, g_fox_q, g_fox_k, w_branch, w_out, g_ffn2, w_ffn2_up,
             w_ffn2_down):
    d, w = cfg.d_model, cfg.width
    nb_p, seq, nb_s, tq_s, past = cfg.batch, cfg.seq, cfg.dec_batch, cfg.dec_seq, cfg.past_len
    ntp, nt = cfg.nt_prompt, cfg.nt
    tm, tmb, tn, t = cfg.tm, cfg.tm_big, cfg.tn, cfg.tq
    assert tq_s == SEG and seq % t == 0 and t >= MAX_DISTANCE and t % CHUNK == 0
    fox_h, diff_h, hg_h = cfg.fox_heads, cfg.diff_heads, cfg.hg_heads
    dff, dffp = cfg.d_ff, cfg.d_ff_pad

    n_seq = nb_p + nb_s
    n_seq_pad = ((n_seq + 7) // 8) * 8
    c_all = jnp.concatenate([c_prompt, c_sample, jnp.zeros((n_seq_pad - n_seq, d), F32)], axis=0)
    mod = ada_mods(c_all, w_ada, b_ada, min(cfg.t_ada, N_ADA * d))
    mod = mod.reshape(cfg.depth, n_seq_pad, N_ADA, d)
    mod_p = jnp.broadcast_to(mod[:, :nb_p, None], (cfg.depth, nb_p, seq // SEG, N_ADA, d))
    mod = jnp.concatenate([mod_p.reshape(cfg.depth, -1, N_ADA, d), mod[:, nb_p:n_seq]], axis=1)
    mod = jnp.transpose(mod, (0, 2, 1, 3))[:, :, :, None, :]

    x3 = jnp.concatenate([x_prompt.reshape(-1, SEG, d), x_sample.reshape(-1, SEG, d)], axis=0)

    cs = jnp.cumsum(jax.nn.softmax(hgrn_lb_logits.astype(F32), axis=0), axis=0)
    lbs = cs - cs[0:1]

    far = rel_bias_table[NUM_BUCKETS // 2 - 1].astype(F32)[:, None, None]
    pos_t = jnp.arange(t, dtype=jnp.int32)
    bias_d, mask_d = _rel_bias(rel_bias_table, pos_t + t, pos_t + t)
    bias_diag = jnp.swapaxes(jnp.where(mask_d[None], bias_d - far, NEG), 1, 2)
    bias_prev = jnp.swapaxes(_rel_bias(rel_bias_table, pos_t + t, pos_t)[0] - far, 1, 2)
    q_pos_s = past + jnp.arange(tq_s, dtype=jnp.int32)
    bias_s, mask_s = _rel_bias(rel_bias_table, q_pos_s, jnp.arange(past + tq_s, dtype=jnp.int32))
    bias_s = jnp.where(mask_s[None], bias_s, NEG)
    bias_s_cache, bias_s_new = bias_s[:, :, :past], bias_s[:, :, past:]

    cache_dk = cache_diff_k.reshape(cfg.depth, nb_s, past * 2 * diff_h, HEAD_DIM)
    cache_dv = cache_diff_v.reshape(cfg.depth, nb_s, past, diff_h, 2, HEAD_DIM)
    cache_dv = jnp.transpose(cache_dv, (0, 1, 2, 4, 3, 5)).reshape(cfg.depth, nb_s, -1, HEAD_DIM)
    cache_fk = cache_fox_k.reshape(cfg.depth, nb_s, past * fox_h, HEAD_DIM)
    cache_fv = cache_fox_v.reshape(cfg.depth, nb_s, past * fox_h, HEAD_DIM)
    zero_state = jnp.zeros((nb_p, hg_h, HEAD_DIM, HEAD_DIM), F32)

    outs = [[] for _ in range(12)]
    for li in range(cfg.depth):
        ml = mod[li]
        def up_parts(w_up):
            pad = ((0, 0), (0, dffp - dff))
            return (jnp.pad(w_up[:, :dff], pad).astype(BF16), jnp.pad(w_up[:, dff:], pad).astype(BF16))

        def down_part(w_down):
            return jnp.pad(w_down, ((0, dffp - dff), (0, 0))).astype(BF16)

        wa1, wb1 = up_parts(w_ffn1_up[li])
        wd1 = down_part(w_ffn1_down[li])
        wa2, wb2 = up_parts(w_ffn2_up[li])
        wd2 = down_part(w_ffn2_down[li])
        w_main = w_in[li][:, :10 * w].astype(BF16)
        w_ff = jnp.pad(w_in[li][:, 10 * w:10 * w + fox_h], ((0, 0), (0, HEAD_DIM - fox_h))).astype(BF16)
        b_ff = jnp.pad(b_fox_f[li].astype(F32), (0, HEAD_DIM - fox_h)).reshape(1, HEAD_DIM)
        w_gates = w_in[li][:, 10 * w + fox_h:].astype(BF16)
        w_br = w_branch[li].astype(BF16)
        w_o = w_out[li].astype(BF16)

        h = norm_mod(x3, g_ffn1[li], ml, 1, 0, tm)
        x3 = mm_residual(ffn_up(h, wa1, wb1, tmb, tn), wd1, x3, ml, 2, 0.5, tmb, tn)

        h = norm_mod(x3, g_mix[li], ml, 4, 3, tm)
        lb = lbs[li].reshape(1, w)
        q_hg, g_hg, k_hg, v_hg, og = proj_hgrn(h, w_main, jnp.log(lb), jnp.log1p(-lb), 1.0 - lb, tm)
        dq, dk_f, dk_b, dv_f, dv_b = proj_attn(h, w_main, 4, g_diff_q[li], g_diff_k[li], tm)
        fq, fk_f, fk_b, fv_f, fv_b, logf = proj_attn(h, w_main, 7, g_fox_q[li], g_fox_k[li], tm,
                                                     w_ff, b_ff)
        gates = proj_gates(h, w_gates, tmb, tn)
        logf = logf[:, :fox_h]

        o_hg_p, st_p = hgrn_mixer(q_hg, k_hg, v_hg, g_hg, og, zero_state, g_hg_o[li], 0, nb_p, seq)
        o_hg_s, st_s = hgrn_mixer(q_hg, k_hg, v_hg, g_hg, og, state_hgrn[li].astype(F32),
                                  g_hg_o[li], ntp, nb_s, tq_s)

        lam_init = 0.8 - 0.6 * math.exp(-0.3 * li)
        dl = diff_lambda[li].astype(F32)
        lam = (jnp.exp(jnp.sum(dl[0] * dl[1])) - jnp.exp(jnp.sum(dl[2] * dl[3])) + lam_init).reshape(1, 1)
        o_df_p = diff_prompt(dq, dk_b, jnp.transpose(dv_b[:ntp]), bias_diag, bias_prev, lam,
                             g_diff_o[li], 1.0 - lam_init, nb_p, seq, t, cfg.diff_group)
        o_df_s = diff_sample(dq, dk_b, dv_b, cache_dk, cache_dv, li, bias_s_cache, bias_s_new, lam,
                             g_diff_o[li], 1.0 - lam_init, ntp, nb_s, tq_s)

        logf_p = logf[:ntp].reshape(nb_p, seq, fox_h)
        logf_s = logf[ntp:].reshape(nb_s, tq_s, fox_h)
        ck_p = jnp.broadcast_to(cumsum_time(logf_p)[..., None], (nb_p, fox_h, seq, HEAD_DIM))
        cum_s = cumsum_time(jnp.concatenate([cache_fox_logf[li].astype(F32), logf_s], axis=1))
        o_fx_p = fox_prompt(fq, fk_b, jnp.transpose(fv_b[:ntp]), ck_p, nb_p, seq, t, cfg.fox_group)
        o_fx_s = fox_sample(fq, fk_b, fv_b, cache_fk, cache_fv, li, cum_s[:, :, :past],
                            cum_s[:, :, past:], ntp, nb_s, tq_s)

        branches = [jnp.concatenate([a, b], axis=0) for a, b in
                    ((o_hg_p, o_hg_s), (o_df_p, o_df_s), (o_fx_p, o_fx_s))]
        merged = merge_branches(branches, gates, w_br, tmb, tn)
        x3 = mm_residual(merged, w_o, x3, ml, 5, 1.0, tmb, tn)

        h = norm_mod(x3, g_ffn2[li], ml, 7, 6, tm)
        x3 = mm_residual(ffn_up(h, wa2, wb2, tmb, tn), wd2, x3, ml, 8, 0.5, tmb, tn)

        for idx, (a, shape_p, shape_s) in enumerate((
                (dk_f, (nb_p, seq, diff_h, 2, HEAD_DIM), (nb_s, tq_s, diff_h, 2, HEAD_DIM)),
                (dv_f, (nb_p, seq, diff_h, 2 * HEAD_DIM), (nb_s, tq_s, diff_h, 2 * HEAD_DIM)),
                (fk_f, (nb_p, seq, fox_h, HEAD_DIM), (nb_s, tq_s, fox_h, HEAD_DIM)),
                (fv_f, (nb_p, seq, fox_h, HEAD_DIM), (nb_s, tq_s, fox_h, HEAD_DIM)),
                (logf, (nb_p, seq, fox_h), (nb_s, tq_s, fox_h)))):
            outs[idx].append(a[:ntp].reshape(shape_p))
            outs[6 + idx].append(a[ntp:].reshape(shape_s))
        outs[5].append(st_p)
        outs[11].append(st_s)

    y_prompt = x3[:ntp // SEG].reshape(nb_p, seq, d)
    y_sample = x3[ntp // SEG:].reshape(nb_s, tq_s, d)
    return (y_prompt, y_sample) + tuple(jnp.stack(o) for o in outs)


def kernel(x_prompt, x_sample, c_prompt, c_sample, cache_diff_k, cache_diff_v, cache_fox_k, cache_fox_v, cache_fox_logf, state_hgrn, rel_bias_table, hgrn_lb_logits, w_ada, b_ada, g_ffn1, w_ffn1_up, w_ffn1_down, g_mix, w_in, b_fox_f, g_hg_o, g_diff_q, g_diff_k, diff_lambda, g_diff_o, g_fox_q, g_fox_k, w_branch, w_out, g_ffn2, w_ffn2_up, w_ffn2_down):
    return _forward(FULL_CFG, x_prompt, x_sample, c_prompt, c_sample, cache_diff_k, cache_diff_v,
                    cache_fox_k, cache_fox_v, cache_fox_logf, state_hgrn, rel_bias_table,
                    hgrn_lb_logits, w_ada, b_ada, g_ffn1, w_ffn1_up, w_ffn1_down, g_mix, w_in,
                    b_fox_f, g_hg_o, g_diff_q, g_diff_k, diff_lambda, g_diff_o, g_fox_q, g_fox_k,
                    w_branch, w_out, g_ffn2, w_ffn2_up, w_ffn2_down)
```

```python
import functools
import math
from typing import NamedTuple

import jax
import jax.numpy as jnp
import numpy as np
from jax import lax
from jax.experimental import pallas as pl
from jax.experimental.pallas import tpu as pltpu

F32 = jnp.float32
BF16 = jnp.bfloat16

HEAD_DIM = 128
SEG = 64
CHUNK = 64
NUM_BUCKETS = 32
MAX_DISTANCE = 128
N_ADA = 9
N_BRANCH = 3
EPS = 1e-6
NEG = -1e30
HG_CHUNK = 128
VMEM_LIMIT_BYTES = 56 * 1024 * 1024


class Cfg(NamedTuple):
    d_model: int
    batch: int
    seq: int
    depth: int
    dec_batch: int
    dec_seq: int
    past_len: int
    tm: int = 512
    tm_big: int = 1024
    tm_up: int = 2048
    tn: int = 512
    tq: int = 512
    fox_group: int = 2
    diff_group: int = 2
    t_ada: int = 1024

    @property
    def width(self):
        return self.d_model // 2

    @property
    def hg_heads(self):
        return self.width // HEAD_DIM

    @property
    def diff_heads(self):
        return self.width // (2 * HEAD_DIM)

    @property
    def fox_heads(self):
        return self.width // HEAD_DIM

    @property
    def d_ff(self):
        return ((8 * self.d_model // 3 + 127) // 128) * 128

    @property
    def d_ff_pad(self):
        return ((self.d_ff + self.tn - 1) // self.tn) * self.tn

    @property
    def nt_prompt(self):
        return self.batch * self.seq

    @property
    def nt_sample(self):
        return self.dec_batch * self.dec_seq

    @property
    def nt(self):
        return self.nt_prompt + self.nt_sample


FULL_CFG = Cfg(d_model=2048, batch=2, seq=4096, depth=4, dec_batch=32, dec_seq=64, past_len=2048)


def _params(*sem):
    return pltpu.CompilerParams(dimension_semantics=sem, vmem_limit_bytes=VMEM_LIMIT_BYTES)


def _silu(x):
    return x * jax.nn.sigmoid(x)


def _log_sigmoid(x):
    return jnp.minimum(x, 0.0) - jnp.log1p(jnp.exp(-jnp.abs(x)))


def _nt_dot(a, b):
    return lax.dot_general(a, b, (((1,), (1,)), ((), ())), preferred_element_type=F32)


def _tn_dot(a, b):
    return lax.dot_general(a, b, (((0,), (0,)), ((), ())), preferred_element_type=F32)


def _split3(x):
    x1 = x.astype(BF16)
    r1 = x - x1.astype(F32)
    x2 = r1.astype(BF16)
    x3 = (r1 - x2.astype(F32)).astype(BF16)
    return x1, x2, x3


def _exact_left_dot(m, x):
    return sum(jnp.dot(m, p, preferred_element_type=F32) for p in _split3(x))


def _ada_kernel(c_ref, w_ref, b_ref, o_ref):
    a = _silu(c_ref[...]).astype(BF16)
    o_ref[...] = jnp.dot(a, w_ref[...].astype(BF16), preferred_element_type=F32) + b_ref[...]


def ada_mods(c_all, w_ada, b_ada, tn):
    depth, d, n = w_ada.shape
    r = c_all.shape[0]
    return pl.pallas_call(
        _ada_kernel,
        grid=(depth, n // tn),
        in_specs=[pl.BlockSpec((r, d), lambda l, j: (0, 0)),
                  pl.BlockSpec((None, d, tn), lambda l, j: (l, 0, j)),
                  pl.BlockSpec((None, 1, tn), lambda l, j: (l, 0, j))],
        out_specs=pl.BlockSpec((None, r, tn), lambda l, j: (l, 0, j)),
        out_shape=jax.ShapeDtypeStruct((depth, r, n), F32),
        compiler_params=_params("parallel", "parallel"),
        name="ada_mods",
    )(c_all, w_ada, b_ada.reshape(depth, 1, n))


def _norm_mod_kernel(x_ref, g_ref, sc_ref, sh_ref, o_ref):
    x = x_ref[...]
    ms = jnp.mean(x * x, axis=-1, keepdims=True)
    y = x * lax.rsqrt(ms + EPS) * g_ref[...]
    y = y * (1.0 + sc_ref[...]) + sh_ref[...]
    o_ref[...] = y.reshape(o_ref.shape).astype(BF16)


def norm_mod(x3, g, mods, i_scale, i_shift, tm):
    ns, seg, d = x3.shape
    s = tm // seg
    return pl.pallas_call(
        _norm_mod_kernel,
        grid=(ns // s,),
        in_specs=[pl.BlockSpec((s, seg, d), lambda i: (i, 0, 0)),
                  pl.BlockSpec((1, d), lambda i: (0, 0)),
                  pl.BlockSpec((None, s, 1, d), lambda i: (i_scale, i, 0, 0)),
                  pl.BlockSpec((None, s, 1, d), lambda i: (i_shift, i, 0, 0))],
        out_specs=pl.BlockSpec((tm, d), lambda i: (i, 0)),
        out_shape=jax.ShapeDtypeStruct((ns * seg, d), BF16),
        compiler_params=_params("parallel"),
        name="norm_mod",
    )(x3, g.reshape(1, d), mods, mods)


def _ffn_up_kernel(h_ref, wa_ref, wb_ref, o_ref):
    h = h_ref[...]
    a = jnp.dot(h, wa_ref[...], preferred_element_type=F32)
    b = jnp.dot(h, wb_ref[...], preferred_element_type=F32)
    o_ref[...] = (_silu(a) * b).astype(BF16)


def ffn_up(h, wa, wb, tm, tn):
    m, k = h.shape
    f = wa.shape[1]
    return pl.pallas_call(
        _ffn_up_kernel,
        grid=(m // tm, f // tn),
        in_specs=[pl.BlockSpec((tm, k), lambda i, j: (i, 0)),
                  pl.BlockSpec((k, tn), lambda i, j: (0, j)),
                  pl.BlockSpec((k, tn), lambda i, j: (0, j))],
        out_specs=pl.BlockSpec((tm, tn), lambda i, j: (i, j)),
        out_shape=jax.ShapeDtypeStruct((m, f), BF16),
        compiler_params=_params("parallel", "arbitrary"),
        name="ffn_up",
    )(h, wa, wb)


def _mm_residual_kernel(a_ref, w_ref, x_ref, gt_ref, o_ref, *, coef):
    acc = jnp.dot(a_ref[...], w_ref[...], preferred_element_type=F32)
    o_ref[...] = x_ref[...] + (coef * gt_ref[...]) * acc.reshape(x_ref.shape)


def mm_residual(a, w, x3, mods, i_gate, coef, tm, tn):
    m, k = a.shape
    n = w.shape[1]
    ns, seg, _ = x3.shape
    s = tm // seg
    return pl.pallas_call(
        functools.partial(_mm_residual_kernel, coef=coef),
        grid=(m // tm, n // tn),
        in_specs=[pl.BlockSpec((tm, k), lambda i, j: (i, 0)),
                  pl.BlockSpec((k, tn), lambda i, j: (0, j)),
                  pl.BlockSpec((s, seg, tn), lambda i, j: (i, 0, j)),
                  pl.BlockSpec((None, s, 1, tn), lambda i, j: (i_gate, i, 0, j))],
        out_specs=pl.BlockSpec((s, seg, tn), lambda i, j: (i, 0, j)),
        out_shape=jax.ShapeDtypeStruct(x3.shape, F32),
        compiler_params=_params("parallel", "arbitrary"),
        name="mm_residual",
    )(a, w, x3, mods)


def _proj_hgrn_kernel(h_ref, w_ref, la_ref, l1m_ref, oml_ref, q_ref, g_ref, k_ref, v_ref, og_ref):
    j = pl.program_id(1)
    acc = jnp.dot(h_ref[...], w_ref[...], preferred_element_type=F32)

    @pl.when(j == 0)
    def _():
        q_ref[...] = (_silu(acc) * (HEAD_DIM ** -0.5)).astype(BF16)

    @pl.when(j == 1)
    def _():
        la = la_ref[...]
        y = l1m_ref[...] + _log_sigmoid(acc)
        g_ref[...] = jnp.maximum(la, y) + jnp.log1p(jnp.exp(-jnp.abs(la - y)))
        k_ref[...] = (oml_ref[...] * jax.nn.sigmoid(-acc)).astype(BF16)

    @pl.when(j == 2)
    def _():
        v_ref[...] = acc.astype(BF16)

    @pl.when(j == 3)
    def _():
        og_ref[...] = jax.nn.sigmoid(acc).astype(BF16)


def proj_hgrn(h, w_main, log_lb, log1m_lb, one_m_lb, tm):
    m, k = h.shape
    w = log_lb.shape[-1]
    row = lambda i, j: (i, 0)
    vec = pl.BlockSpec((1, w), lambda i, j: (0, 0))
    return pl.pallas_call(
        _proj_hgrn_kernel,
        grid=(m // tm, 4),
        in_specs=[pl.BlockSpec((tm, k), row),
                  pl.BlockSpec((k, w), lambda i, j: (0, j)),
                  vec, vec, vec],
        out_specs=[pl.BlockSpec((tm, w), row)] * 5,
        out_shape=[jax.ShapeDtypeStruct((m, w), BF16),
                   jax.ShapeDtypeStruct((m, w), F32),
                   jax.ShapeDtypeStruct((m, w), BF16),
                   jax.ShapeDtypeStruct((m, w), BF16),
                   jax.ShapeDtypeStruct((m, w), BF16)],
        compiler_params=_params("parallel", "arbitrary"),
        name="proj_hgrn",
    )(h, w_main, log_lb, log1m_lb, one_m_lb)


def _head_rms(acc, g):
    outs = []
    for c in range(acc.shape[1] // HEAD_DIM):
        ch = acc[:, c * HEAD_DIM:(c + 1) * HEAD_DIM]
        ms = jnp.mean(ch * ch, axis=-1, keepdims=True)
        outs.append(ch * lax.rsqrt(ms + EPS) * g)
    return outs


def _proj_attn_kernel(*refs, q_scale, with_forget, n_alias):
    n_in = (6 if with_forget else 4) + n_alias
    h_ref, w_ref, gq_ref, gk_ref = refs[:4]
    if with_forget:
        wf_ref, bf_ref = refs[4:6]
        qb_ref, kb_ref, vb_ref, kf_ref, vf_ref, lf_ref = refs[n_in:]
    else:
        qb_ref, kb_ref, vb_ref, kf_ref, vf_ref = refs[n_in:]
    j = pl.program_id(1)
    acc = jnp.dot(h_ref[...], w_ref[...], preferred_element_type=F32)

    @pl.when(j == 0)
    def _():
        for c, ch in enumerate(_head_rms(acc, gq_ref[...])):
            qb_ref[:, c * HEAD_DIM:(c + 1) * HEAD_DIM] = (ch * q_scale).astype(BF16)
        if with_forget:
            f = jnp.dot(h_ref[...], wf_ref[...], preferred_element_type=F32) + bf_ref[...]
            lf_ref[...] = _log_sigmoid(f)

    @pl.when(j == 1)
    def _():
        for c, ch in enumerate(_head_rms(acc, gk_ref[...])):
            kf_ref[:, c * HEAD_DIM:(c + 1) * HEAD_DIM] = ch
            kb_ref[:, c * HEAD_DIM:(c + 1) * HEAD_DIM] = ch.astype(BF16)

    @pl.when(j == 2)
    def _():
        vf_ref[...] = acc
        vb_ref[...] = acc.astype(BF16)


def proj_attn(h, w_main, col_block, g_q, g_k, tm, row0, nrows, layer, depth, stacks,
              w_forget=None, b_forget=None):
    k = h.shape[1]
    w = w_main.shape[1] // 10
    with_forget = w_forget is not None
    blk0 = row0 // tm
    row = lambda i, j: (i, 0)
    vec = pl.BlockSpec((1, HEAD_DIM), lambda i, j: (0, 0))
    in_specs = [pl.BlockSpec((tm, k), lambda i, j: (blk0 + i, 0)),
                pl.BlockSpec((k, w), lambda i, j: (0, col_block + j)),
                vec, vec]
    args = [h, w_main, g_q.reshape(1, HEAD_DIM), g_k.reshape(1, HEAD_DIM)]
    if with_forget:
        in_specs += [pl.BlockSpec((k, HEAD_DIM), lambda i, j: (0, 0)), vec]
        args += [w_forget, b_forget]
    widths = [w, w] + ([HEAD_DIM] if with_forget else [])
    out_specs = [pl.BlockSpec((tm, w), row)] * 3
    out_shape = [jax.ShapeDtypeStruct((nrows, w), BF16)] * 3
    out_specs += [pl.BlockSpec((None, tm, wd), lambda i, j: (layer, i, 0)) for wd in widths]
    out_shape += [jax.ShapeDtypeStruct((depth, nrows, wd), F32) for wd in widths]
    aliases = {}
    if stacks is not None:
        aliases = {len(args) + n: 3 + n for n in range(len(widths))}
        in_specs += [pl.BlockSpec(memory_space=pl.ANY)] * len(widths)
        args += list(stacks)
    res = pl.pallas_call(
        functools.partial(_proj_attn_kernel, q_scale=HEAD_DIM ** -0.5, with_forget=with_forget,
                          n_alias=len(aliases)),
        grid=(nrows // tm, 3),
        in_specs=in_specs, out_specs=out_specs, out_shape=out_shape,
        input_output_aliases=aliases,
        compiler_params=_params("parallel", "arbitrary"),
        name="proj_fox" if with_forget else "proj_diff",
    )(*args)
    return res[:3], res[3:]


def _proj_gates_kernel(h_ref, w_ref, o_ref):
    acc = jnp.dot(h_ref[...], w_ref[...], preferred_element_type=F32)
    o_ref[...] = jax.nn.sigmoid(acc).astype(BF16)


def proj_gates(h, w_gates, tm, tn):
    m, k = h.shape
    n = w_gates.shape[1]
    return pl.pallas_call(
        _proj_gates_kernel,
        grid=(m // tm, n // tn),
        in_specs=[pl.BlockSpec((tm, k), lambda i, j: (i, 0)),
                  pl.BlockSpec((k, tn), lambda i, j: (0, j))],
        out_specs=pl.BlockSpec((tm, tn), lambda i, j: (i, j)),
        out_shape=jax.ShapeDtypeStruct((m, n), BF16),
        compiler_params=_params("parallel", "arbitrary"),
        name="proj_gates",
    )(h, w_gates)


def _merge_kernel(b0_ref, b1_ref, b2_ref, g0_ref, g1_ref, g2_ref, w_ref, o_ref):
    acc = None
    for n, (b_ref, g_ref) in enumerate(((b0_ref, g0_ref), (b1_ref, g1_ref), (b2_ref, g2_ref))):
        up = jnp.dot(b_ref[...], w_ref[n], preferred_element_type=F32)
        t = g_ref[...].astype(F32) * up
        acc = t if acc is None else acc + t
    o_ref[...] = acc.astype(BF16)


def merge_branches(branches, gates, w_branch, tm, tn):
    m, w = branches[0].shape
    d = w_branch.shape[2]
    nj = d // tn
    bspec = pl.BlockSpec((tm, w), lambda i, j: (i, 0))
    gspecs = [pl.BlockSpec((tm, tn), functools.partial(lambda i, j, n: (i, n * nj + j), n=n))
              for n in range(N_BRANCH)]
    return pl.pallas_call(
        _merge_kernel,
        grid=(m // tm, nj),
        in_specs=[bspec, bspec, bspec] + gspecs +
                 [pl.BlockSpec((N_BRANCH, w, tn), lambda i, j: (0, 0, j))],
        out_specs=pl.BlockSpec((tm, tn), lambda i, j: (i, j)),
        out_shape=jax.ShapeDtypeStruct((m, d), BF16),
        compiler_params=_params("parallel", "arbitrary"),
        name="merge_branches",
    )(*branches, gates, gates, gates, w_branch)


def _hgrn_level_tables(c):
    t = np.arange(c)[:, None]
    u = np.arange(c)[None, :]
    sums = [(u <= t)]
    masks = [(t == u)]
    w = 1
    while w < c:
        upper = (t // w) % 2 == 1
        q_sum = upper & (u >= (t // w) * w) & (u <= t)
        k_sum = (~upper) & (u > t) & (u <= (t // w) * w + w - 1)
        sums.append(q_sum | k_sum)
        masks.append((t // (2 * w) == u // (2 * w)) & upper & ((u // w) % 2 == 0))
        w *= 2
    return (np.stack(sums).astype(np.float32).reshape(-1, c), np.stack(masks).astype(np.float32))


def _hgrn_kernel(q_ref, k_ref, v_ref, g_ref, og_ref, s0_ref, gn_ref, sums_ref, masks_ref,
                 o_ref, sout_ref, st_ref, *, heads, rows):
    c = pl.program_id(1)
    last = pl.num_programs(1) - 1
    C = HG_CHUNK
    levels = masks_ref.shape[0] - 1

    @pl.when(c == 0)
    def _():
        for h in range(heads):
            st_ref[h] = s0_ref[h].T

    gn = gn_ref[...]

    def pad_rows(x):
        if rows == C:
            return x
        return jnp.concatenate([x, jnp.zeros((C - rows, x.shape[1]), x.dtype)], axis=0)

    qb = pad_rows(q_ref[...])
    kb = pad_rows(k_ref[...])
    v = pad_rows(v_ref[...])
    q = qb.astype(F32)
    k = kb.astype(F32)
    g1, g2, g3 = _split3(pad_rows(g_ref[...]))
    sums = (jnp.dot(sums_ref[...], g1, preferred_element_type=F32)
            + jnp.dot(sums_ref[...], g2, preferred_element_type=F32))
    b = sums[:C] + jnp.dot(sums_ref[:C, :], g3, preferred_element_type=F32)
    qe = (q * jnp.exp(b)).astype(BF16)
    b_last = b[C - 1:C, :]
    kd = (k * jnp.exp(b_last - b)).astype(BF16)
    decay = jnp.exp(b_last)
    cols = [slice(h * HEAD_DIM, (h + 1) * HEAD_DIM) for h in range(heads)]
    att = [masks_ref[0] * _nt_dot(qb[:, s], kb[:, s]) for s in cols]
    for lv in range(1, levels + 1):
        e = jnp.exp(sums[lv * C:(lv + 1) * C])
        qw = (q * e).astype(BF16)
        kw = (k * e).astype(BF16)
        mask = masks_ref[lv]
        for h, s in enumerate(cols):
            att[h] = att[h] + mask * _nt_dot(qw[:, s], kw[:, s])
    for h, s in enumerate(cols):
        st = st_ref[h]
        o = (_nt_dot(qe[:, s], st.astype(BF16))
             + jnp.dot(att[h].astype(BF16), v[:, s], preferred_element_type=F32))
        st_ref[h] = st * decay[:, s] + _tn_dot(v[:, s], kd[:, s])
        o = o[:rows]
        ms = jnp.mean(o * o, axis=-1, keepdims=True)
        o = o * lax.rsqrt(ms + EPS) * gn * og_ref[:, s].astype(F32)
        o_ref[:, s] = o.astype(BF16)

    @pl.when(c == last)
    def _():
        for h in range(heads):
            sout_ref[h] = st_ref[h].T


def hgrn_mixer(q, k, v, g, og, s0, g_norm, row0, nb, t):
    w = q.shape[1]
    heads = w // HEAD_DIM
    rows = min(t, HG_CHUNK)
    nc = t // rows
    blk0 = row0 // rows
    sums, masks = _hgrn_level_tables(HG_CHUNK)
    tok = pl.BlockSpec((rows, w), lambda b, c: (blk0 + b * nc + c, 0))
    state = pl.BlockSpec((None, heads, HEAD_DIM, HEAD_DIM), lambda b, c: (b, 0, 0, 0))
    return pl.pallas_call(
        functools.partial(_hgrn_kernel, heads=heads, rows=rows),
        grid=(nb, nc),
        in_specs=[tok, tok, tok, tok, tok, state, pl.BlockSpec((1, HEAD_DIM), lambda b, c: (0, 0)),
                  pl.BlockSpec(sums.shape, lambda b, c: (0, 0)),
                  pl.BlockSpec(masks.shape, lambda b, c: (0, 0, 0))],
        out_specs=[pl.BlockSpec((rows, w), lambda b, c: (b * nc + c, 0)), state],
        out_shape=[jax.ShapeDtypeStruct((nb * t, w), BF16),
                   jax.ShapeDtypeStruct((nb, heads, HEAD_DIM, HEAD_DIM), F32)],
        scratch_shapes=[pltpu.VMEM((heads, HEAD_DIM, HEAD_DIM), F32)],
        compiler_params=_params("parallel", "arbitrary"),
        name="hgrn_mixer",
    )(q, k, v, g, og, s0, g_norm.reshape(1, HEAD_DIM), jnp.asarray(sums, BF16), jnp.asarray(masks))


def _cumsum_kernel(x_ref, o_ref):
    rows, n, lanes = x_ref.shape
    ki = lax.broadcasted_iota(jnp.int32, (lanes, lanes), 0)
    ji = lax.broadcasted_iota(jnp.int32, (lanes, lanes), 1)
    upper = (ki <= ji).astype(BF16)
    lower = (ji < ki).astype(BF16)
    ones = jnp.ones((lanes, lanes), BF16)
    for r in range(rows):
        parts = _split3(x_ref[r])
        within = sum(jnp.dot(p, upper, preferred_element_type=F32) for p in parts)
        total = sum(jnp.dot(p, ones, preferred_element_type=F32) for p in parts)
        total = jnp.concatenate([total, jnp.zeros((lanes - n, lanes), F32)], axis=0)
        o_ref[r] = within + _exact_left_dot(lower, total)[:n]


def cumsum_time(x):
    b, t, h = x.shape
    lanes = HEAD_DIM
    tp = -(-t // (8 * lanes)) * (8 * lanes)
    n = tp // lanes
    assert n <= lanes
    xt = jnp.pad(jnp.transpose(x, (0, 2, 1)), ((0, 0), (0, 0), (0, tp - t))).reshape(b * h, n, lanes)
    rows = 8 if (b * h) % 8 == 0 else b * h
    y = pl.pallas_call(
        _cumsum_kernel,
        grid=(b * h // rows,),
        in_specs=[pl.BlockSpec((rows, n, lanes), lambda i: (i, 0, 0))],
        out_specs=pl.BlockSpec((rows, n, lanes), lambda i: (i, 0, 0)),
        out_shape=jax.ShapeDtypeStruct((b * h, n, lanes), F32),
        compiler_params=_params("parallel"),
        name="cumsum_time",
    )(xt)
    return y.reshape(b, h, tp)[:, :, :t]


def _softmax_init(m_ref, l_ref, acc_ref):
    m_ref[...] = jnp.full(m_ref.shape, NEG, F32)
    l_ref[...] = jnp.zeros(l_ref.shape, F32)
    acc_ref[...] = jnp.zeros(acc_ref.shape, F32)


def _softmax_step_t(st, vt, m_ref, l_ref, acc_ref, idx):
    m_old = m_ref[idx]
    m_new = jnp.maximum(m_old, jnp.max(st, axis=0, keepdims=True))
    alpha = jnp.exp(m_old - m_new)
    p = jnp.exp(st - m_new)
    l_ref[idx] = alpha * l_ref[idx] + jnp.sum(p, axis=0, keepdims=True)
    acc_ref[idx] = alpha * acc_ref[idx] + jnp.dot(vt, p.astype(BF16), preferred_element_type=F32)
    m_ref[idx] = m_new


def _fox_prompt_kernel(q_ref, k_ref, vt_ref, ck_ref, o_ref, m_ref, l_ref, acc_ref, *, t, group):
    qi = pl.program_id(2)
    _softmax_init(m_ref, l_ref, acc_ref)
    reps = t // HEAD_DIM

    def tile(kt, mask):
        rows = pl.ds(pl.multiple_of(kt * t, t), t)
        for h in range(group):
            cols = slice(h * HEAD_DIM, (h + 1) * HEAD_DIM)
            ck = ck_ref[h, rows, :]
            st = _nt_dot(k_ref[rows, cols], q_ref[:, cols]) - jnp.concatenate([ck] * reps, axis=1)
            if mask is not None:
                st = jnp.where(mask, st, NEG)
            _softmax_step_t(st, vt_ref[cols, rows], m_ref, l_ref, acc_ref, h)

    def body(kt, carry):
        tile(kt, None)
        return carry

    lax.fori_loop(0, qi, body, 0)
    key = lax.broadcasted_iota(jnp.int32, (t, t), 0)
    query = lax.broadcasted_iota(jnp.int32, (t, t), 1)
    tile(qi, key <= query)
    for h in range(group):
        o_ref[:, h * HEAD_DIM:(h + 1) * HEAD_DIM] = (acc_ref[h] / l_ref[h]).T.astype(BF16)


def fox_prompt(q, k, vt, ck, nb, seq, t, group):
    heads = q.shape[1] // HEAD_DIM
    group = min(group, heads)
    gw = group * HEAD_DIM
    nq = seq // t
    return pl.pallas_call(
        functools.partial(_fox_prompt_kernel, t=t, group=group),
        grid=(nb, heads // group, nq),
        in_specs=[pl.BlockSpec((t, gw), lambda b, h, i: (b * nq + i, h)),
                  pl.BlockSpec((seq, gw), lambda b, h, i: (b, h)),
                  pl.BlockSpec((gw, seq), lambda b, h, i: (h, b)),
                  pl.BlockSpec((None, group, seq, HEAD_DIM), lambda b, h, i: (b, h, 0, 0))],
        out_specs=pl.BlockSpec((t, gw), lambda b, h, i: (b * nq + i, h)),
        out_shape=jax.ShapeDtypeStruct((nb * seq, heads * HEAD_DIM), BF16),
        scratch_shapes=[pltpu.VMEM((group, 1, t), F32), pltpu.VMEM((group, 1, t), F32),
                        pltpu.VMEM((group, HEAD_DIM, t), F32)],
        compiler_params=_params("parallel", "parallel", "arbitrary"),
        name="fox_prompt",
    )(q, k, vt, ck)


def _diff_prompt_kernel(q_ref, k_ref, vt_ref, b0_ref, b1_ref, lam_ref, go_ref, o_ref,
                        m_ref, l_ref, acc_ref, *, t, group, out_scale):
    qi = pl.program_id(2)
    dv = 2 * HEAD_DIM
    _softmax_init(m_ref, l_ref, acc_ref)

    def tile(kt, bias_ref):
        rows = pl.ds(pl.multiple_of(kt * t, t), t)
        for h in range(group):
            vt = vt_ref[h * dv:(h + 1) * dv, rows]
            for c in range(2):
                cols = slice((2 * h + c) * HEAD_DIM, (2 * h + c + 1) * HEAD_DIM)
                st = _nt_dot(k_ref[rows, cols], q_ref[:, cols])
                if bias_ref is not None:
                    st = st + bias_ref[h]
                _softmax_step_t(st, vt, m_ref, l_ref, acc_ref, 2 * h + c)

    def body(kt, carry):
        tile(kt, None)
        return carry

    lax.fori_loop(0, jnp.maximum(qi - 1, 0), body, 0)

    @pl.when(qi >= 1)
    def _():
        tile(qi - 1, b1_ref)

    tile(qi, b0_ref)
    lam = lam_ref[...]
    for h in range(group):
        ot = acc_ref[2 * h] / l_ref[2 * h] - lam * (acc_ref[2 * h + 1] / l_ref[2 * h + 1])
        ms = jnp.mean(ot * ot, axis=0, keepdims=True)
        ot = ot * lax.rsqrt(ms + EPS) * (go_ref[...] * out_scale)
        o_ref[:, h * dv:(h + 1) * dv] = ot.T.astype(BF16)


def diff_prompt(q, k, vt, bias_diag, bias_prev, lam, g_o, out_scale, nb, seq, t, group):
    dv = 2 * HEAD_DIM
    heads = q.shape[1] // dv
    group = min(group, heads)
    gw = group * dv
    nq = seq // t
    bias = pl.BlockSpec((group, t, t), lambda b, h, i: (h, 0, 0))
    return pl.pallas_call(
        functools.partial(_diff_prompt_kernel, t=t, group=group, out_scale=out_scale),
        grid=(nb, heads // group, nq),
        in_specs=[pl.BlockSpec((t, gw), lambda b, h, i: (b * nq + i, h)),
                  pl.BlockSpec((seq, gw), lambda b, h, i: (b, h)),
                  pl.BlockSpec((gw, seq), lambda b, h, i: (h, b)),
                  bias, bias,
                  pl.BlockSpec((1, 1), lambda b, h, i: (0, 0)),
                  pl.BlockSpec((dv, 1), lambda b, h, i: (0, 0))],
        out_specs=pl.BlockSpec((t, gw), lambda b, h, i: (b * nq + i, h)),
        out_shape=jax.ShapeDtypeStruct((nb * seq, heads * dv), BF16),
        scratch_shapes=[pltpu.VMEM((2 * group, 1, t), F32), pltpu.VMEM((2 * group, 1, t), F32),
                        pltpu.VMEM((2 * group, dv, t), F32)],
        compiler_params=_params("parallel", "parallel", "arbitrary"),
        name="diff_prompt",
    )(q, k, vt, bias_diag, bias_prev, lam, g_o.reshape(dv, 1))


def _two_block_softmax(s_cache, s_new, v_cache, v_new):
    m = jnp.maximum(jnp.max(s_cache, axis=-1, keepdims=True), jnp.max(s_new, axis=-1, keepdims=True))
    p_cache = jnp.exp(s_cache - m)
    p_new = jnp.exp(s_new - m)
    l = jnp.sum(p_cache, axis=-1, keepdims=True) + jnp.sum(p_new, axis=-1, keepdims=True)
    o = (jnp.dot(p_cache.astype(BF16), v_cache, preferred_element_type=F32)
         + jnp.dot(p_new.astype(BF16), v_new, preferred_element_type=F32))
    return o / l


def _fox_sample_kernel(q_ref, kc_ref, vc_ref, ckc_ref, kn_ref, vn_ref, ckn_ref, o_ref, *, heads):
    tq = q_ref.shape[0]
    past = kc_ref.shape[0] // heads
    ri = lax.broadcasted_iota(jnp.int32, (tq, tq), 0)
    ci = lax.broadcasted_iota(jnp.int32, (tq, tq), 1)
    for h in range(heads):
        cols = slice(h * HEAD_DIM, (h + 1) * HEAD_DIM)
        head_rows = pl.ds(h, past, stride=heads)
        q = q_ref[:, cols]
        s_cache = _nt_dot(q, kc_ref[head_rows, :].astype(BF16)) - ckc_ref[h:h + 1, :]
        s_new = jnp.where(ci <= ri, _nt_dot(q, kn_ref[:, cols]) - ckn_ref[h:h + 1, :], NEG)
        o = _two_block_softmax(s_cache, s_new, vc_ref[head_rows, :].astype(BF16), vn_ref[:, cols])
        o_ref[:, cols] = o.astype(BF16)


def fox_sample(q, k, v, cache_k, cache_v, layer, ck_cache, ck_new, row0, nb, tq):
    w = q.shape[1]
    heads = w // HEAD_DIM
    past = cache_k.shape[2] // heads
    blk0 = row0 // tq
    new = pl.BlockSpec((tq, w), lambda b: (blk0 + b, 0))
    cache = pl.BlockSpec((None, None, past * heads, HEAD_DIM), lambda b: (layer, b, 0, 0))
    return pl.pallas_call(
        functools.partial(_fox_sample_kernel, heads=heads),
        grid=(nb,),
        in_specs=[new, cache, cache,
                  pl.BlockSpec((None, heads, past), lambda b: (b, 0, 0)),
                  new, new,
                  pl.BlockSpec((None, heads, tq), lambda b: (b, 0, 0))],
        out_specs=pl.BlockSpec((tq, w), lambda b: (b, 0)),
        out_shape=jax.ShapeDtypeStruct((nb * tq, w), BF16),
        compiler_params=_params("parallel"),
        name="fox_sample",
    )(q, cache_k, cache_v, ck_cache, k, v, ck_new)


def _diff_sample_kernel(q_ref, kc_ref, vc_ref, bc_ref, kn_ref, vn_ref, bn_ref, lam_ref, go_ref,
                        o_ref, *, heads, out_scale):
    dv = 2 * HEAD_DIM
    past = vc_ref.shape[0] // (2 * heads)
    lam = lam_ref[...]
    for h in range(heads):
        halves = [vc_ref[pl.ds(half * heads + h, past, stride=2 * heads), :] for half in range(2)]
        v_cache = jnp.concatenate(halves, axis=1).astype(BF16)
        v_new = vn_ref[:, h * dv:(h + 1) * dv]
        maps = []
        for c in range(2):
            cols = slice((2 * h + c) * HEAD_DIM, (2 * h + c + 1) * HEAD_DIM)
            map_rows = pl.ds(2 * h + c, past, stride=2 * heads)
            q = q_ref[:, cols]
            s_cache = _nt_dot(q, kc_ref[map_rows, :].astype(BF16)) + bc_ref[h]
            s_new = _nt_dot(q, kn_ref[:, cols]) + bn_ref[h]
            maps.append(_two_block_softmax(s_cache, s_new, v_cache, v_new))
        o = maps[0] - lam * maps[1]
        ms = jnp.mean(o * o, axis=-1, keepdims=True)
        o_ref[:, h * dv:(h + 1) * dv] = (o * lax.rsqrt(ms + EPS) * (go_ref[...] * out_scale)).astype(BF16)


def diff_sample(q, k, v, cache_k, cache_v, layer, bias_cache, bias_new, lam, g_o, out_scale,
                row0, nb, tq):
    w = q.shape[1]
    dv = 2 * HEAD_DIM
    heads = w // dv
    past = cache_v.shape[2] // (2 * heads)
    blk0 = row0 // tq
    new = pl.BlockSpec((tq, w), lambda b: (blk0 + b, 0))
    cache = pl.BlockSpec((None, None, past * 2 * heads, HEAD_DIM), lambda b: (layer, b, 0, 0))
    return pl.pallas_call(
        functools.partial(_diff_sample_kernel, heads=heads, out_scale=out_scale),
        grid=(nb,),
        in_specs=[new, cache, cache,
                  pl.BlockSpec((heads, tq, past), lambda b: (0, 0, 0)),
                  new, new,
                  pl.BlockSpec((heads, tq, tq), lambda b: (0, 0, 0)),
                  pl.BlockSpec((1, 1), lambda b: (0, 0)),
                  pl.BlockSpec((1, dv), lambda b: (0, 0))],
        out_specs=pl.BlockSpec((tq, w), lambda b: (b, 0)),
        out_shape=jax.ShapeDtypeStruct((nb * tq, w), BF16),
        compiler_params=_params("parallel"),
        name="diff_sample",
    )(q, cache_k, cache_v, bias_cache, k, v, bias_new, lam, g_o.reshape(1, dv))


def _t5_bucket(rel):
    half = NUM_BUCKETS // 2
    max_exact = half // 2
    ret = jnp.where(rel > 0, half, 0)
    n = jnp.abs(rel)
    nf = jnp.maximum(n, 1).astype(F32)
    large = max_exact + (jnp.log(nf / max_exact) / math.log(MAX_DISTANCE / max_exact)
                         * (half - max_exact)).astype(jnp.int32)
    large = jnp.minimum(large, half - 1)
    return ret + jnp.where(n < max_exact, n, large)


def _rel_bias(table, q_pos, k_pos):
    bucket = _t5_bucket(k_pos[None, :] - q_pos[:, None])[None]
    table = table.astype(F32)
    bias = sum(jnp.where(bucket == b, table[b][:, None, None], 0.0) for b in range(NUM_BUCKETS))
    mask = (k_pos[None, :] // CHUNK) <= (q_pos[:, None] // CHUNK)
    return bias, mask


def _forward(cfg, x_prompt, x_sample, c_prompt, c_sample, cache_diff_k, cache_diff_v, cache_fox_k,
             cache_fox_v, cache_fox_logf, state_hgrn, rel_bias_table, hgrn_lb_logits, w_ada, b_ada,
             g_ffn1, w_ffn1_up, w_ffn1_down, g_mix, w_in, b_fox_f, g_hg_o, g_diff_q, g_diff_k,
             diff_lambda, g_diff_o, g_fox_q, g_fox_k, w_branch, w_out, g_ffn2, w_ffn2_up,
             w_ffn2_down):
    d, w = cfg.d_model, cfg.width
    nb_p, seq, nb_s, tq_s, past = cfg.batch, cfg.seq, cfg.dec_batch, cfg.dec_seq, cfg.past_len
    ntp, nt = cfg.nt_prompt, cfg.nt
    tm, tmb, tn, t = cfg.tm, cfg.tm_big, cfg.tn, cfg.tq
    assert tq_s == SEG and seq % t == 0 and t >= MAX_DISTANCE and t % CHUNK == 0
    fox_h, diff_h, hg_h = cfg.fox_heads, cfg.diff_heads, cfg.hg_heads
    dff, dffp = cfg.d_ff, cfg.d_ff_pad

    n_seq = nb_p + nb_s
    n_seq_pad = ((n_seq + 7) // 8) * 8
    c_all = jnp.concatenate([c_prompt, c_sample, jnp.zeros((n_seq_pad - n_seq, d), F32)], axis=0)
    mod = ada_mods(c_all, w_ada, b_ada, min(cfg.t_ada, N_ADA * d))
    mod = mod.reshape(cfg.depth, n_seq_pad, N_ADA, d)
    mod_p = jnp.broadcast_to(mod[:, :nb_p, None], (cfg.depth, nb_p, seq // SEG, N_ADA, d))
    mod = jnp.concatenate([mod_p.reshape(cfg.depth, -1, N_ADA, d), mod[:, nb_p:n_seq]], axis=1)
    mod = jnp.transpose(mod, (0, 2, 1, 3))[:, :, :, None, :]

    x3 = jnp.concatenate([x_prompt.reshape(-1, SEG, d), x_sample.reshape(-1, SEG, d)], axis=0)

    cs = jnp.cumsum(jax.nn.softmax(hgrn_lb_logits.astype(F32), axis=0), axis=0)
    lbs = cs - cs[0:1]

    far = rel_bias_table[NUM_BUCKETS // 2 - 1].astype(F32)[:, None, None]
    pos_t = jnp.arange(t, dtype=jnp.int32)
    bias_d, mask_d = _rel_bias(rel_bias_table, pos_t + t, pos_t + t)
    bias_diag = jnp.swapaxes(jnp.where(mask_d[None], bias_d - far, NEG), 1, 2)
    bias_prev = jnp.swapaxes(_rel_bias(rel_bias_table, pos_t + t, pos_t)[0] - far, 1, 2)
    q_pos_s = past + jnp.arange(tq_s, dtype=jnp.int32)
    bias_s, mask_s = _rel_bias(rel_bias_table, q_pos_s, jnp.arange(past + tq_s, dtype=jnp.int32))
    bias_s = jnp.where(mask_s[None], bias_s, NEG)
    bias_s_cache, bias_s_new = bias_s[:, :, :past], bias_s[:, :, past:]

    cache_dk = cache_diff_k.reshape(cfg.depth, nb_s, past * 2 * diff_h, HEAD_DIM)
    cache_dv = cache_diff_v.reshape(cfg.depth, nb_s, past, diff_h, 2, HEAD_DIM)
    cache_dv = jnp.transpose(cache_dv, (0, 1, 2, 4, 3, 5)).reshape(cfg.depth, nb_s, -1, HEAD_DIM)
    cache_fk = cache_fox_k.reshape(cfg.depth, nb_s, past * fox_h, HEAD_DIM)
    cache_fv = cache_fox_v.reshape(cfg.depth, nb_s, past * fox_h, HEAD_DIM)
    zero_state = jnp.zeros((nb_p, hg_h, HEAD_DIM, HEAD_DIM), F32)

    diff_stacks = {"p": None, "s": None}
    fox_stacks = {"p": None, "s": None}
    states = {"p": [], "s": []}
    for li in range(cfg.depth):
        ml = mod[li]
        def up_parts(w_up):
            pad = ((0, 0), (0, dffp - dff))
            return (jnp.pad(w_up[:, :dff], pad).astype(BF16), jnp.pad(w_up[:, dff:], pad).astype(BF16))

        def down_part(w_down):
            return jnp.pad(w_down, ((0, dffp - dff), (0, 0))).astype(BF16)

        wa1, wb1 = up_parts(w_ffn1_up[li])
        wd1 = down_part(w_ffn1_down[li])
        wa2, wb2 = up_parts(w_ffn2_up[li])
        wd2 = down_part(w_ffn2_down[li])
        w_main = w_in[li][:, :10 * w].astype(BF16)
        w_ff = jnp.pad(w_in[li][:, 10 * w:10 * w + fox_h], ((0, 0), (0, HEAD_DIM - fox_h))).astype(BF16)
        b_ff = jnp.pad(b_fox_f[li].astype(F32), (0, HEAD_DIM - fox_h)).reshape(1, HEAD_DIM)
        w_gates = w_in[li][:, 10 * w + fox_h:].astype(BF16)
        w_br = w_branch[li].astype(BF16)
        w_o = w_out[li].astype(BF16)

        h = norm_mod(x3, g_ffn1[li], ml, 1, 0, tm)
        x3 = mm_residual(ffn_up(h, wa1, wb1, cfg.tm_up, tn), wd1, x3, ml, 2, 0.5, tmb, tn)

        h = norm_mod(x3, g_mix[li], ml, 4, 3, tm)
        lb = lbs[li].reshape(1, w)
        q_hg, g_hg, k_hg, v_hg, og = proj_hgrn(h, w_main, jnp.log(lb), jnp.log1p(-lb), 1.0 - lb, tm)
        groups = {}
        for name, row0, nrows in (("p", 0, ntp), ("s", ntp, nt - ntp)):
            diff_qkv, diff_stacks[name] = proj_attn(
                h, w_main, 4, g_diff_q[li], g_diff_k[li], tm, row0, nrows, li, cfg.depth,
                diff_stacks[name])
            fox_qkv, fox_stacks[name] = proj_attn(
                h, w_main, 7, g_fox_q[li], g_fox_k[li], tm, row0, nrows, li, cfg.depth,
                fox_stacks[name], w_ff, b_ff)
            groups[name] = diff_qkv + fox_qkv
        dq_p, dk_p, dv_p, fq_p, fk_p, fv_p = groups["p"]
        dq_s, dk_s, dv_s, fq_s, fk_s, fv_s = groups["s"]
        gates = proj_gates(h, w_gates, cfg.tm_up, tn)

        o_hg_p, st_p = hgrn_mixer(q_hg, k_hg, v_hg, g_hg, og, zero_state, g_hg_o[li], 0, nb_p, seq)
        o_hg_s, st_s = hgrn_mixer(q_hg, k_hg, v_hg, g_hg, og, state_hgrn[li].astype(F32),
                                  g_hg_o[li], ntp, nb_s, tq_s)

        lam_init = 0.8 - 0.6 * math.exp(-0.3 * li)
        dl = diff_lambda[li].astype(F32)
        lam = (jnp.exp(jnp.sum(dl[0] * dl[1])) - jnp.exp(jnp.sum(dl[2] * dl[3])) + lam_init).reshape(1, 1)
        o_df_p = diff_prompt(dq_p, dk_p, jnp.transpose(dv_p), bias_diag, bias_prev, lam,
                             g_diff_o[li], 1.0 - lam_init, nb_p, seq, t, cfg.diff_group)
        o_df_s = diff_sample(dq_s, dk_s, dv_s, cache_dk, cache_dv, li, bias_s_cache, bias_s_new, lam,
                             g_diff_o[li], 1.0 - lam_init, 0, nb_s, tq_s)

        logf_p = fox_stacks["p"][2][li, :, :fox_h].reshape(nb_p, seq, fox_h)
        logf_s = fox_stacks["s"][2][li, :, :fox_h].reshape(nb_s, tq_s, fox_h)
        ck_p = jnp.broadcast_to(cumsum_time(logf_p)[..., None], (nb_p, fox_h, seq, HEAD_DIM))
        cum_s = cumsum_time(jnp.concatenate([cache_fox_logf[li].astype(F32), logf_s], axis=1))
        o_fx_p = fox_prompt(fq_p, fk_p, jnp.transpose(fv_p), ck_p, nb_p, seq, t, cfg.fox_group)
        o_fx_s = fox_sample(fq_s, fk_s, fv_s, cache_fk, cache_fv, li, cum_s[:, :, :past],
                            cum_s[:, :, past:], 0, nb_s, tq_s)

        branches = [jnp.concatenate([a, b], axis=0) for a, b in
                    ((o_hg_p, o_hg_s), (o_df_p, o_df_s), (o_fx_p, o_fx_s))]
        merged = merge_branches(branches, gates, w_br, tmb, tn)
        x3 = mm_residual(merged, w_o, x3, ml, 5, 1.0, tmb, tn)

        h = norm_mod(x3, g_ffn2[li], ml, 7, 6, tm)
        x3 = mm_residual(ffn_up(h, wa2, wb2, cfg.tm_up, tn), wd2, x3, ml, 8, 0.5, tmb, tn)

        states["p"].append(st_p)
        states["s"].append(st_s)

    def group_outputs(name, nb, tlen):
        lead = (cfg.depth, nb, tlen)
        dk, dv = diff_stacks[name]
        fk, fv, lf = fox_stacks[name]
        return (dk.reshape(lead + (diff_h, 2, HEAD_DIM)), dv.reshape(lead + (diff_h, 2 * HEAD_DIM)),
                fk.reshape(lead + (fox_h, HEAD_DIM)), fv.reshape(lead + (fox_h, HEAD_DIM)),
                lf[:, :, :fox_h].reshape(lead + (fox_h,)), jnp.stack(states[name]))

    y_prompt = x3[:ntp // SEG].reshape(nb_p, seq, d)
    y_sample = x3[ntp // SEG:].reshape(nb_s, tq_s, d)
    return (y_prompt, y_sample) + group_outputs("p", nb_p, seq) + group_outputs("s", nb_s, tq_s)


def kernel(x_prompt, x_sample, c_prompt, c_sample, cache_diff_k, cache_diff_v, cache_fox_k, cache_fox_v, cache_fox_logf, state_hgrn, rel_bias_table, hgrn_lb_logits, w_ada, b_ada, g_ffn1, w_ffn1_up, w_ffn1_down, g_mix, w_in, b_fox_f, g_hg_o, g_diff_q, g_diff_k, diff_lambda, g_diff_o, g_fox_q, g_fox_k, w_branch, w_out, g_ffn2, w_ffn2_up, w_ffn2_down):
    return _forward(FULL_CFG, x_prompt, x_sample, c_prompt, c_sample, cache_diff_k, cache_diff_v,
                    cache_fox_k, cache_fox_v, cache_fox_logf, state_hgrn, rel_bias_table,
                    hgrn_lb_logits, w_ada, b_ada, g_ffn1, w_ffn1_up, w_ffn1_down, g_mix, w_in,
                    b_fox_f, g_hg_o, g_diff_q, g_diff_k, diff_lambda, g_diff_o, g_fox_q, g_fox_k,
                    w_branch, w_out, g_ffn2, w_ffn2_up, w_ffn2_down)
```

```python
import functools
import math
from typing import NamedTuple

import jax
import jax.numpy as jnp
import numpy as np
from jax import lax
from jax.experimental import pallas as pl
from jax.experimental.pallas import tpu as pltpu

F32 = jnp.float32
BF16 = jnp.bfloat16

HEAD_DIM = 128
SEG = 64
CHUNK = 64
NUM_BUCKETS = 32
MAX_DISTANCE = 128
N_ADA = 9
N_BRANCH = 3
EPS = 1e-6
NEG = -1e30
HG_CHUNK = 128
VMEM_LIMIT_BYTES = 56 * 1024 * 1024


class Cfg(NamedTuple):
    d_model: int
    batch: int
    seq: int
    depth: int
    dec_batch: int
    dec_seq: int
    past_len: int
    tm: int = 512
    tm_big: int = 1024
    tm_up: int = 2048
    tn: int = 512
    tq: int = 512
    fox_group: int = 2
    diff_group: int = 2
    t_ada: int = 1024

    @property
    def width(self):
        return self.d_model // 2

    @property
    def hg_heads(self):
        return self.width // HEAD_DIM

    @property
    def diff_heads(self):
        return self.width // (2 * HEAD_DIM)

    @property
    def fox_heads(self):
        return self.width // HEAD_DIM

    @property
    def d_ff(self):
        return ((8 * self.d_model // 3 + 127) // 128) * 128

    @property
    def d_ff_pad(self):
        return ((self.d_ff + self.tn - 1) // self.tn) * self.tn

    @property
    def nt_prompt(self):
        return self.batch * self.seq

    @property
    def nt_sample(self):
        return self.dec_batch * self.dec_seq

    @property
    def nt(self):
        return self.nt_prompt + self.nt_sample


FULL_CFG = Cfg(d_model=2048, batch=2, seq=4096, depth=4, dec_batch=32, dec_seq=64, past_len=2048)


def _params(*sem):
    return pltpu.CompilerParams(dimension_semantics=sem, vmem_limit_bytes=VMEM_LIMIT_BYTES)


def _silu(x):
    return x * jax.nn.sigmoid(x)


def _log_sigmoid(x):
    return jnp.minimum(x, 0.0) - jnp.log1p(jnp.exp(-jnp.abs(x)))


def _nt_dot(a, b):
    return lax.dot_general(a, b, (((1,), (1,)), ((), ())), preferred_element_type=F32)


def _tn_dot(a, b):
    return lax.dot_general(a, b, (((0,), (0,)), ((), ())), preferred_element_type=F32)


def _split3(x):
    x1 = x.astype(BF16)
    r1 = x - x1.astype(F32)
    x2 = r1.astype(BF16)
    x3 = (r1 - x2.astype(F32)).astype(BF16)
    return x1, x2, x3


def _exact_left_dot(m, x):
    return sum(jnp.dot(m, p, preferred_element_type=F32) for p in _split3(x))


def _ada_kernel(c_ref, w_ref, b_ref, o_ref):
    a = _silu(c_ref[...]).astype(BF16)
    o_ref[...] = jnp.dot(a, w_ref[...].astype(BF16), preferred_element_type=F32) + b_ref[...]


def ada_mods(c_all, w_ada, b_ada, tn):
    depth, d, n = w_ada.shape
    r = c_all.shape[0]
    return pl.pallas_call(
        _ada_kernel,
        grid=(depth, n // tn),
        in_specs=[pl.BlockSpec((r, d), lambda l, j: (0, 0)),
                  pl.BlockSpec((None, d, tn), lambda l, j: (l, 0, j)),
                  pl.BlockSpec((None, 1, tn), lambda l, j: (l, 0, j))],
        out_specs=pl.BlockSpec((None, r, tn), lambda l, j: (l, 0, j)),
        out_shape=jax.ShapeDtypeStruct((depth, r, n), F32),
        compiler_params=_params("parallel", "parallel"),
        name="ada_mods",
    )(c_all, w_ada, b_ada.reshape(depth, 1, n))


def _norm_mod_kernel(x_ref, g_ref, sc_ref, sh_ref, o_ref):
    x = x_ref[...]
    ms = jnp.mean(x * x, axis=-1, keepdims=True)
    y = x * lax.rsqrt(ms + EPS) * g_ref[...]
    y = y * (1.0 + sc_ref[...]) + sh_ref[...]
    o_ref[...] = y.reshape(o_ref.shape).astype(BF16)


def norm_mod(x3, g, mods, i_scale, i_shift, tm):
    ns, seg, d = x3.shape
    s = tm // seg
    return pl.pallas_call(
        _norm_mod_kernel,
        grid=(ns // s,),
        in_specs=[pl.BlockSpec((s, seg, d), lambda i: (i, 0, 0)),
                  pl.BlockSpec((1, d), lambda i: (0, 0)),
                  pl.BlockSpec((None, s, 1, d), lambda i: (i_scale, i, 0, 0)),
                  pl.BlockSpec((None, s, 1, d), lambda i: (i_shift, i, 0, 0))],
        out_specs=pl.BlockSpec((tm, d), lambda i: (i, 0)),
        out_shape=jax.ShapeDtypeStruct((ns * seg, d), BF16),
        compiler_params=_params("parallel"),
        name="norm_mod",
    )(x3, g.reshape(1, d), mods, mods)


def _ffn_up_kernel(x_ref, g_ref, sc_ref, sh_ref, wa_ref, wb_ref, o_ref, h_ref):
    @pl.when(pl.program_id(1) == 0)
    def _():
        x = x_ref[...]
        ms = jnp.mean(x * x, axis=-1, keepdims=True)
        y = x * lax.rsqrt(ms + EPS) * g_ref[...]
        y = y * (1.0 + sc_ref[...]) + sh_ref[...]
        h_ref[...] = y.reshape(h_ref.shape).astype(BF16)

    h = h_ref[...]
    a = jnp.dot(h, wa_ref[...], preferred_element_type=F32)
    b = jnp.dot(h, wb_ref[...], preferred_element_type=F32)
    o_ref[...] = (_silu(a) * b).astype(BF16)


def ffn_up(x3, g, mods, i_scale, i_shift, w_up, tm, tn):
    ns, seg, d = x3.shape
    s = tm // seg
    f = w_up.shape[1] // 2
    nj = pl.cdiv(f, tn)
    wb_spec = pl.BlockSpec((pl.Element(d), pl.Element(tn, (0, nj * tn - f))),
                           lambda i, j: (0, pl.multiple_of(f + j * tn, HEAD_DIM)))
    return pl.pallas_call(
        _ffn_up_kernel,
        grid=(ns // s, nj),
        in_specs=[pl.BlockSpec((s, seg, d), lambda i, j: (i, 0, 0)),
                  pl.BlockSpec((1, d), lambda i, j: (0, 0)),
                  pl.BlockSpec((None, s, 1, d), lambda i, j: (i_scale, i, 0, 0)),
                  pl.BlockSpec((None, s, 1, d), lambda i, j: (i_shift, i, 0, 0)),
                  pl.BlockSpec((d, tn), lambda i, j: (0, j)),
                  wb_spec],
        out_specs=pl.BlockSpec((tm, tn), lambda i, j: (i, j)),
        out_shape=jax.ShapeDtypeStruct((ns * seg, f), BF16),
        scratch_shapes=[pltpu.VMEM((tm, d), BF16)],
        compiler_params=_params("parallel", "arbitrary"),
        name="ffn_up",
    )(x3, g.reshape(1, d), mods, mods, w_up, w_up)


def _mm_residual_kernel(a_ref, w_ref, x_ref, gt_ref, o_ref, *, coef):
    acc = jnp.dot(a_ref[...], w_ref[...], preferred_element_type=F32)
    o_ref[...] = x_ref[...] + (coef * gt_ref[...]) * acc.reshape(x_ref.shape)


def mm_residual(a, w, x3, mods, i_gate, coef, tm, tn):
    m, k = a.shape
    n = w.shape[1]
    ns, seg, _ = x3.shape
    s = tm // seg
    return pl.pallas_call(
        functools.partial(_mm_residual_kernel, coef=coef),
        grid=(m // tm, n // tn),
        in_specs=[pl.BlockSpec((tm, k), lambda i, j: (i, 0)),
                  pl.BlockSpec((k, tn), lambda i, j: (0, j)),
                  pl.BlockSpec((s, seg, tn), lambda i, j: (i, 0, j)),
                  pl.BlockSpec((None, s, 1, tn), lambda i, j: (i_gate, i, 0, j))],
        out_specs=pl.BlockSpec((s, seg, tn), lambda i, j: (i, 0, j)),
        out_shape=jax.ShapeDtypeStruct(x3.shape, F32),
        compiler_params=_params("parallel", "arbitrary"),
        name="mm_residual",
    )(a, w, x3, mods)


def _proj_hgrn_kernel(h_ref, w_ref, la_ref, l1m_ref, oml_ref, q_ref, g_ref, k_ref, v_ref, og_ref):
    j = pl.program_id(1)
    acc = jnp.dot(h_ref[...], w_ref[...], preferred_element_type=F32)

    @pl.when(j == 0)
    def _():
        q_ref[...] = (_silu(acc) * (HEAD_DIM ** -0.5)).astype(BF16)

    @pl.when(j == 1)
    def _():
        la = la_ref[...]
        y = l1m_ref[...] + _log_sigmoid(acc)
        g_ref[...] = jnp.maximum(la, y) + jnp.log1p(jnp.exp(-jnp.abs(la - y)))
        k_ref[...] = (oml_ref[...] * jax.nn.sigmoid(-acc)).astype(BF16)

    @pl.when(j == 2)
    def _():
        v_ref[...] = acc.astype(BF16)

    @pl.when(j == 3)
    def _():
        og_ref[...] = jax.nn.sigmoid(acc).astype(BF16)


def proj_hgrn(h, w_main, log_lb, log1m_lb, one_m_lb, tm):
    m, k = h.shape
    w = log_lb.shape[-1]
    row = lambda i, j: (i, 0)
    vec = pl.BlockSpec((1, w), lambda i, j: (0, 0))
    return pl.pallas_call(
        _proj_hgrn_kernel,
        grid=(m // tm, 4),
        in_specs=[pl.BlockSpec((tm, k), row),
                  pl.BlockSpec((k, w), lambda i, j: (0, j)),
                  vec, vec, vec],
        out_specs=[pl.BlockSpec((tm, w), row)] * 5,
        out_shape=[jax.ShapeDtypeStruct((m, w), BF16),
                   jax.ShapeDtypeStruct((m, w), F32),
                   jax.ShapeDtypeStruct((m, w), BF16),
                   jax.ShapeDtypeStruct((m, w), BF16),
                   jax.ShapeDtypeStruct((m, w), BF16)],
        compiler_params=_params("parallel", "arbitrary"),
        name="proj_hgrn",
    )(h, w_main, log_lb, log1m_lb, one_m_lb)


def _head_rms(acc, g):
    outs = []
    for c in range(acc.shape[1] // HEAD_DIM):
        ch = acc[:, c * HEAD_DIM:(c + 1) * HEAD_DIM]
        ms = jnp.mean(ch * ch, axis=-1, keepdims=True)
        outs.append(ch * lax.rsqrt(ms + EPS) * g)
    return outs


def _proj_attn_kernel(*refs, q_scale, with_forget, n_alias, head_rows):
    n_in = (6 if with_forget else 4) + n_alias
    h_ref, w_ref, gq_ref, gk_ref = refs[:4]
    if with_forget:
        wf_ref, bf_ref = refs[4:6]
        qb_ref, kb_ref, vb_ref, kf_ref, vf_ref, lf_ref = refs[n_in:]
    else:
        qb_ref, kb_ref, vb_ref, kf_ref, vf_ref = refs[n_in:]
    j = pl.program_id(1)
    tm = h_ref.shape[0]
    acc = jnp.dot(h_ref[...], w_ref[...], preferred_element_type=F32)
    n_chunks = acc.shape[1] // HEAD_DIM

    def chunk(c):
        return slice(c * HEAD_DIM, (c + 1) * HEAD_DIM)

    @pl.when(j == 0)
    def _():
        for c, ch in enumerate(_head_rms(acc, gq_ref[...])):
            qb_ref[:, chunk(c)] = (ch * q_scale).astype(BF16)
        if with_forget:
            f = jnp.dot(h_ref[...], wf_ref[...], preferred_element_type=F32) + bf_ref[...]
            lf_ref[...] = _log_sigmoid(f)

    @pl.when(j == 1)
    def _():
        for c, ch in enumerate(_head_rms(acc, gk_ref[...])):
            if head_rows:
                kf_ref[pl.ds(c, tm, stride=n_chunks), :] = ch
            else:
                kf_ref[:, chunk(c)] = ch
            kb_ref[:, chunk(c)] = ch.astype(BF16)

    @pl.when(j == 2)
    def _():
        if head_rows:
            for c in range(n_chunks):
                row = (c % 2) * (n_chunks // 2) + c // 2
                vf_ref[pl.ds(row, tm, stride=n_chunks), :] = acc[:, chunk(c)]
        else:
            vf_ref[...] = acc
        vb_ref[...] = acc.astype(BF16)


def proj_attn(h, w_in, w, col_block, g_q, g_k, tm, row0, nrows, layer, depth, stacks,
              head_rows=False, forget_block=None, b_forget=None):
    k = h.shape[1]
    with_forget = forget_block is not None
    blk0 = row0 // tm
    row = lambda i, j: (i, 0)
    vec = pl.BlockSpec((1, HEAD_DIM), lambda i, j: (0, 0))
    in_specs = [pl.BlockSpec((tm, k), lambda i, j: (blk0 + i, 0)),
                pl.BlockSpec((k, w), lambda i, j: (0, col_block + j)),
                vec, vec]
    args = [h, w_in, g_q.reshape(1, HEAD_DIM), g_k.reshape(1, HEAD_DIM)]
    if with_forget:
        in_specs += [pl.BlockSpec((k, HEAD_DIM), lambda i, j: (0, forget_block)), vec]
        args += [w_in, b_forget]
    n_chunks = w // HEAD_DIM
    state_shape = (nrows * n_chunks, HEAD_DIM) if head_rows else (nrows, w)
    state_block = (tm * n_chunks, HEAD_DIM) if head_rows else (tm, w)
    states = [(state_shape, state_block)] * 2
    if with_forget:
        states.append(((nrows, HEAD_DIM), (tm, HEAD_DIM)))
    out_specs = [pl.BlockSpec((tm, w), row)] * 3
    out_shape = [jax.ShapeDtypeStruct((nrows, w), BF16)] * 3
    out_specs += [pl.BlockSpec((None,) + blk, lambda i, j: (layer, i, 0)) for _, blk in states]
    out_shape += [jax.ShapeDtypeStruct((depth,) + shp, F32) for shp, _ in states]
    aliases = {}
    if stacks is not None:
        aliases = {len(args) + n: 3 + n for n in range(len(states))}
        in_specs += [pl.BlockSpec(memory_space=pl.ANY)] * len(states)
        args += list(stacks)
    res = pl.pallas_call(
        functools.partial(_proj_attn_kernel, q_scale=HEAD_DIM ** -0.5, with_forget=with_forget,
                          n_alias=len(aliases), head_rows=head_rows),
        grid=(nrows // tm, 3),
        in_specs=in_specs, out_specs=out_specs, out_shape=out_shape,
        input_output_aliases=aliases,
        compiler_params=_params("parallel", "arbitrary"),
        name="proj_fox" if with_forget else "proj_diff",
    )(*args)
    return res[:3], res[3:]


def _proj_gates_kernel(h_ref, w_ref, o_ref):
    acc = jnp.dot(h_ref[...], w_ref[...], preferred_element_type=F32)
    o_ref[...] = jax.nn.sigmoid(acc).astype(BF16)


def proj_gates(h, w_gates, tm, tn):
    m, k = h.shape
    n = w_gates.shape[1]
    return pl.pallas_call(
        _proj_gates_kernel,
        grid=(m // tm, n // tn),
        in_specs=[pl.BlockSpec((tm, k), lambda i, j: (i, 0)),
                  pl.BlockSpec((k, tn), lambda i, j: (0, j))],
        out_specs=pl.BlockSpec((tm, tn), lambda i, j: (i, j)),
        out_shape=jax.ShapeDtypeStruct((m, n), BF16),
        compiler_params=_params("parallel", "arbitrary"),
        name="proj_gates",
    )(h, w_gates)


def _merge_kernel(b0_ref, b1_ref, b2_ref, g0_ref, g1_ref, g2_ref, w_ref, o_ref):
    acc = None
    for n, (b_ref, g_ref) in enumerate(((b0_ref, g0_ref), (b1_ref, g1_ref), (b2_ref, g2_ref))):
        up = jnp.dot(b_ref[...], w_ref[n], preferred_element_type=F32)
        t = g_ref[...].astype(F32) * up
        acc = t if acc is None else acc + t
    o_ref[...] = acc.astype(BF16)


def merge_branches(branches, gates, w_branch, tm, tn):
    m, w = branches[0].shape
    d = w_branch.shape[2]
    nj = d // tn
    bspec = pl.BlockSpec((tm, w), lambda i, j: (i, 0))
    gspecs = [pl.BlockSpec((tm, tn), functools.partial(lambda i, j, n: (i, n * nj + j), n=n))
              for n in range(N_BRANCH)]
    return pl.pallas_call(
        _merge_kernel,
        grid=(m // tm, nj),
        in_specs=[bspec, bspec, bspec] + gspecs +
                 [pl.BlockSpec((N_BRANCH, w, tn), lambda i, j: (0, 0, j))],
        out_specs=pl.BlockSpec((tm, tn), lambda i, j: (i, j)),
        out_shape=jax.ShapeDtypeStruct((m, d), BF16),
        compiler_params=_params("parallel", "arbitrary"),
        name="merge_branches",
    )(*branches, gates, gates, gates, w_branch)


def _hgrn_level_tables(c):
    t = np.arange(c)[:, None]
    u = np.arange(c)[None, :]
    sums = [(u <= t)]
    masks = [(t == u)]
    w = 1
    while w < c:
        upper = (t // w) % 2 == 1
        q_sum = upper & (u >= (t // w) * w) & (u <= t)
        k_sum = (~upper) & (u > t) & (u <= (t // w) * w + w - 1)
        sums.append(q_sum | k_sum)
        masks.append((t // (2 * w) == u // (2 * w)) & upper & ((u // w) % 2 == 0))
        w *= 2
    return (np.stack(sums).astype(np.float32).reshape(-1, c), np.stack(masks).astype(np.float32))


def _hgrn_kernel(q_ref, k_ref, v_ref, g_ref, og_ref, s0_ref, gn_ref, sums_ref, masks_ref,
                 o_ref, sout_ref, st_ref, *, heads, rows):
    c = pl.program_id(1)
    last = pl.num_programs(1) - 1
    C = HG_CHUNK
    levels = masks_ref.shape[0] - 1

    @pl.when(c == 0)
    def _():
        for h in range(heads):
            st_ref[h] = s0_ref[h].T

    gn = gn_ref[...]

    def pad_rows(x):
        if rows == C:
            return x
        return jnp.concatenate([x, jnp.zeros((C - rows, x.shape[1]), x.dtype)], axis=0)

    qb = pad_rows(q_ref[...])
    kb = pad_rows(k_ref[...])
    v = pad_rows(v_ref[...])
    q = qb.astype(F32)
    k = kb.astype(F32)
    g1, g2, g3 = _split3(pad_rows(g_ref[...]))
    sums = (jnp.dot(sums_ref[...], g1, preferred_element_type=F32)
            + jnp.dot(sums_ref[...], g2, preferred_element_type=F32))
    b = sums[:C] + jnp.dot(sums_ref[:C, :], g3, preferred_element_type=F32)
    qe = (q * jnp.exp(b)).astype(BF16)
    b_last = b[C - 1:C, :]
    kd = (k * jnp.exp(b_last - b)).astype(BF16)
    decay = jnp.exp(b_last)
    cols = [slice(h * HEAD_DIM, (h + 1) * HEAD_DIM) for h in range(heads)]
    att = [masks_ref[0] * _nt_dot(qb[:, s], kb[:, s]) for s in cols]
    for lv in range(1, levels + 1):
        e = jnp.exp(sums[lv * C:(lv + 1) * C])
        qw = (q * e).astype(BF16)
        kw = (k * e).astype(BF16)
        mask = masks_ref[lv]
        for h, s in enumerate(cols):
            att[h] = att[h] + mask * _nt_dot(qw[:, s], kw[:, s])
    for h, s in enumerate(cols):
        st = st_ref[h]
        o = (_nt_dot(qe[:, s], st.astype(BF16))
             + jnp.dot(att[h].astype(BF16), v[:, s], preferred_element_type=F32))
        st_ref[h] = st * decay[:, s] + _tn_dot(v[:, s], kd[:, s])
        o = o[:rows]
        ms = jnp.mean(o * o, axis=-1, keepdims=True)
        o = o * lax.rsqrt(ms + EPS) * gn * og_ref[:, s].astype(F32)
        o_ref[:, s] = o.astype(BF16)

    @pl.when(c == last)
    def _():
        for h in range(heads):
            sout_ref[h] = st_ref[h].T


def hgrn_mixer(q, k, v, g, og, s0, g_norm, row0, nb, t):
    w = q.shape[1]
    heads = w // HEAD_DIM
    rows = min(t, HG_CHUNK)
    nc = t // rows
    blk0 = row0 // rows
    sums, masks = _hgrn_level_tables(HG_CHUNK)
    tok = pl.BlockSpec((rows, w), lambda b, c: (blk0 + b * nc + c, 0))
    state = pl.BlockSpec((None, heads, HEAD_DIM, HEAD_DIM), lambda b, c: (b, 0, 0, 0))
    return pl.pallas_call(
        functools.partial(_hgrn_kernel, heads=heads, rows=rows),
        grid=(nb, nc),
        in_specs=[tok, tok, tok, tok, tok, state, pl.BlockSpec((1, HEAD_DIM), lambda b, c: (0, 0)),
                  pl.BlockSpec(sums.shape, lambda b, c: (0, 0)),
                  pl.BlockSpec(masks.shape, lambda b, c: (0, 0, 0))],
        out_specs=[pl.BlockSpec((rows, w), lambda b, c: (b * nc + c, 0)), state],
        out_shape=[jax.ShapeDtypeStruct((nb * t, w), BF16),
                   jax.ShapeDtypeStruct((nb, heads, HEAD_DIM, HEAD_DIM), F32)],
        scratch_shapes=[pltpu.VMEM((heads, HEAD_DIM, HEAD_DIM), F32)],
        compiler_params=_params("parallel", "arbitrary"),
        name="hgrn_mixer",
    )(q, k, v, g, og, s0, g_norm.reshape(1, HEAD_DIM), jnp.asarray(sums, BF16), jnp.asarray(masks))


def _cumsum_kernel(x_ref, o_ref):
    rows, n, lanes = x_ref.shape
    ki = lax.broadcasted_iota(jnp.int32, (lanes, lanes), 0)
    ji = lax.broadcasted_iota(jnp.int32, (lanes, lanes), 1)
    upper = (ki <= ji).astype(BF16)
    lower = (ji < ki).astype(BF16)
    ones = jnp.ones((lanes, lanes), BF16)
    for r in range(rows):
        parts = _split3(x_ref[r])
        within = sum(jnp.dot(p, upper, preferred_element_type=F32) for p in parts)
        total = sum(jnp.dot(p, ones, preferred_element_type=F32) for p in parts)
        total = jnp.concatenate([total, jnp.zeros((lanes - n, lanes), F32)], axis=0)
        o_ref[r] = within + _exact_left_dot(lower, total)[:n]


def cumsum_time(x):
    b, t, h = x.shape
    lanes = HEAD_DIM
    tp = -(-t // (8 * lanes)) * (8 * lanes)
    n = tp // lanes
    assert n <= lanes
    xt = jnp.pad(jnp.transpose(x, (0, 2, 1)), ((0, 0), (0, 0), (0, tp - t))).reshape(b * h, n, lanes)
    rows = 8 if (b * h) % 8 == 0 else b * h
    y = pl.pallas_call(
        _cumsum_kernel,
        grid=(b * h // rows,),
        in_specs=[pl.BlockSpec((rows, n, lanes), lambda i: (i, 0, 0))],
        out_specs=pl.BlockSpec((rows, n, lanes), lambda i: (i, 0, 0)),
        out_shape=jax.ShapeDtypeStruct((b * h, n, lanes), F32),
        compiler_params=_params("parallel"),
        name="cumsum_time",
    )(xt)
    return y.reshape(b, h, tp)[:, :, :t]


def _softmax_init(m_ref, l_ref, acc_ref):
    m_ref[...] = jnp.full(m_ref.shape, NEG, F32)
    l_ref[...] = jnp.zeros(l_ref.shape, F32)
    acc_ref[...] = jnp.zeros(acc_ref.shape, F32)


def _softmax_step_t(st, vt, m_ref, l_ref, acc_ref, idx):
    m_old = m_ref[idx]
    m_new = jnp.maximum(m_old, jnp.max(st, axis=0, keepdims=True))
    alpha = jnp.exp(m_old - m_new)
    p = jnp.exp(st - m_new)
    l_ref[idx] = alpha * l_ref[idx] + jnp.sum(p, axis=0, keepdims=True)
    acc_ref[idx] = alpha * acc_ref[idx] + jnp.dot(vt, p.astype(BF16), preferred_element_type=F32)
    m_ref[idx] = m_new


def _fox_prompt_kernel(q_ref, k_ref, vt_ref, ck_ref, o_ref, m_ref, l_ref, acc_ref, *, t, group):
    qi = pl.program_id(2)
    _softmax_init(m_ref, l_ref, acc_ref)
    reps = t // HEAD_DIM

    def tile(kt, mask):
        rows = pl.ds(pl.multiple_of(kt * t, t), t)
        for h in range(group):
            cols = slice(h * HEAD_DIM, (h + 1) * HEAD_DIM)
            ck = ck_ref[h, rows, :]
            st = _nt_dot(k_ref[rows, cols], q_ref[:, cols]) - jnp.concatenate([ck] * reps, axis=1)
            if mask is not None:
                st = jnp.where(mask, st, NEG)
            _softmax_step_t(st, vt_ref[cols, rows], m_ref, l_ref, acc_ref, h)

    def body(kt, carry):
        tile(kt, None)
        return carry

    lax.fori_loop(0, qi, body, 0)
    key = lax.broadcasted_iota(jnp.int32, (t, t), 0)
    query = lax.broadcasted_iota(jnp.int32, (t, t), 1)
    tile(qi, key <= query)
    for h in range(group):
        o_ref[:, h * HEAD_DIM:(h + 1) * HEAD_DIM] = (acc_ref[h] / l_ref[h]).T.astype(BF16)


def fox_prompt(q, k, vt, ck, nb, seq, t, group):
    heads = q.shape[1] // HEAD_DIM
    group = min(group, heads)
    gw = group * HEAD_DIM
    nq = seq // t
    return pl.pallas_call(
        functools.partial(_fox_prompt_kernel, t=t, group=group),
        grid=(nb, heads // group, nq),
        in_specs=[pl.BlockSpec((t, gw), lambda b, h, i: (b * nq + i, h)),
                  pl.BlockSpec((seq, gw), lambda b, h, i: (b, h)),
                  pl.BlockSpec((gw, seq), lambda b, h, i: (h, b)),
                  pl.BlockSpec((None, group, seq, HEAD_DIM), lambda b, h, i: (b, h, 0, 0))],
        out_specs=pl.BlockSpec((t, gw), lambda b, h, i: (b * nq + i, h)),
        out_shape=jax.ShapeDtypeStruct((nb * seq, heads * HEAD_DIM), BF16),
        scratch_shapes=[pltpu.VMEM((group, 1, t), F32), pltpu.VMEM((group, 1, t), F32),
                        pltpu.VMEM((group, HEAD_DIM, t), F32)],
        compiler_params=_params("parallel", "parallel", "arbitrary"),
        name="fox_prompt",
    )(q, k, vt, ck)


def _diff_prompt_kernel(q_ref, k_ref, vt_ref, b0_ref, b1_ref, lam_ref, go_ref, o_ref,
                        m_ref, l_ref, acc_ref, *, t, group, out_scale):
    qi = pl.program_id(2)
    dv = 2 * HEAD_DIM
    _softmax_init(m_ref, l_ref, acc_ref)

    def tile(kt, bias_ref):
        rows = pl.ds(pl.multiple_of(kt * t, t), t)
        for h in range(group):
            vt = vt_ref[h * dv:(h + 1) * dv, rows]
            for c in range(2):
                cols = slice((2 * h + c) * HEAD_DIM, (2 * h + c + 1) * HEAD_DIM)
                st = _nt_dot(k_ref[rows, cols], q_ref[:, cols])
                if bias_ref is not None:
                    st = st + bias_ref[h]
                _softmax_step_t(st, vt, m_ref, l_ref, acc_ref, 2 * h + c)

    def body(kt, carry):
        tile(kt, None)
        return carry

    lax.fori_loop(0, jnp.maximum(qi - 1, 0), body, 0)

    @pl.when(qi >= 1)
    def _():
        tile(qi - 1, b1_ref)

    tile(qi, b0_ref)
    lam = lam_ref[...]
    for h in range(group):
        ot = acc_ref[2 * h] / l_ref[2 * h] - lam * (acc_ref[2 * h + 1] / l_ref[2 * h + 1])
        ms = jnp.mean(ot * ot, axis=0, keepdims=True)
        ot = ot * lax.rsqrt(ms + EPS) * (go_ref[...] * out_scale)
        o_ref[:, h * dv:(h + 1) * dv] = ot.T.astype(BF16)


def diff_prompt(q, k, vt, bias_diag, bias_prev, lam, g_o, out_scale, nb, seq, t, group):
    dv = 2 * HEAD_DIM
    heads = q.shape[1] // dv
    group = min(group, heads)
    gw = group * dv
    nq = seq // t
    bias = pl.BlockSpec((group, t, t), lambda b, h, i: (h, 0, 0))
    return pl.pallas_call(
        functools.partial(_diff_prompt_kernel, t=t, group=group, out_scale=out_scale),
        grid=(nb, heads // group, nq),
        in_specs=[pl.BlockSpec((t, gw), lambda b, h, i: (b * nq + i, h)),
                  pl.BlockSpec((seq, gw), lambda b, h, i: (b, h)),
                  pl.BlockSpec((gw, seq), lambda b, h, i: (h, b)),
                  bias, bias,
                  pl.BlockSpec((1, 1), lambda b, h, i: (0, 0)),
                  pl.BlockSpec((dv, 1), lambda b, h, i: (0, 0))],
        out_specs=pl.BlockSpec((t, gw), lambda b, h, i: (b * nq + i, h)),
        out_shape=jax.ShapeDtypeStruct((nb * seq, heads * dv), BF16),
        scratch_shapes=[pltpu.VMEM((2 * group, 1, t), F32), pltpu.VMEM((2 * group, 1, t), F32),
                        pltpu.VMEM((2 * group, dv, t), F32)],
        compiler_params=_params("parallel", "parallel", "arbitrary"),
        name="diff_prompt",
    )(q, k, vt, bias_diag, bias_prev, lam, g_o.reshape(dv, 1))


def _two_block_softmax(s_cache, s_new, v_cache, v_new):
    m = jnp.maximum(jnp.max(s_cache, axis=-1, keepdims=True), jnp.max(s_new, axis=-1, keepdims=True))
    p_cache = jnp.exp(s_cache - m)
    p_new = jnp.exp(s_new - m)
    l = jnp.sum(p_cache, axis=-1, keepdims=True) + jnp.sum(p_new, axis=-1, keepdims=True)
    o = (jnp.dot(p_cache.astype(BF16), v_cache, preferred_element_type=F32)
         + jnp.dot(p_new.astype(BF16), v_new, preferred_element_type=F32))
    return o / l


def _fox_sample_kernel(q_ref, kc_ref, vc_ref, ckc_ref, kn_ref, vn_ref, ckn_ref, o_ref, *, heads):
    tq = q_ref.shape[0]
    past = kc_ref.shape[0] // heads
    ri = lax.broadcasted_iota(jnp.int32, (tq, tq), 0)
    ci = lax.broadcasted_iota(jnp.int32, (tq, tq), 1)
    for h in range(heads):
        cols = slice(h * HEAD_DIM, (h + 1) * HEAD_DIM)
        head_rows = pl.ds(h, past, stride=heads)
        q = q_ref[:, cols]
        s_cache = _nt_dot(q, kc_ref[head_rows, :].astype(BF16)) - ckc_ref[h:h + 1, :]
        s_new = jnp.where(ci <= ri, _nt_dot(q, kn_ref[:, cols]) - ckn_ref[h:h + 1, :], NEG)
        o = _two_block_softmax(s_cache, s_new, vc_ref[head_rows, :].astype(BF16), vn_ref[:, cols])
        o_ref[:, cols] = o.astype(BF16)


def fox_sample(q, k, v, cache_k, cache_v, layer, ck_cache, ck_new, row0, nb, tq):
    w = q.shape[1]
    heads = w // HEAD_DIM
    past = cache_k.shape[2] // heads
    blk0 = row0 // tq
    new = pl.BlockSpec((tq, w), lambda b: (blk0 + b, 0))
    cache = pl.BlockSpec((None, None, past * heads, HEAD_DIM), lambda b: (layer, b, 0, 0))
    return pl.pallas_call(
        functools.partial(_fox_sample_kernel, heads=heads),
        grid=(nb,),
        in_specs=[new, cache, cache,
                  pl.BlockSpec((None, heads, past), lambda b: (b, 0, 0)),
                  new, new,
                  pl.BlockSpec((None, heads, tq), lambda b: (b, 0, 0))],
        out_specs=pl.BlockSpec((tq, w), lambda b: (b, 0)),
        out_shape=jax.ShapeDtypeStruct((nb * tq, w), BF16),
        compiler_params=_params("parallel"),
        name="fox_sample",
    )(q, cache_k, cache_v, ck_cache, k, v, ck_new)


def _diff_sample_kernel(q_ref, kc_ref, vc_ref, bc_ref, kn_ref, vn_ref, bn_ref, lam_ref, go_ref,
                        o_ref, *, heads, out_scale):
    dv = 2 * HEAD_DIM
    past = vc_ref.shape[0] // (2 * heads)
    lam = lam_ref[...]
    for h in range(heads):
        halves = [vc_ref[pl.ds(half * heads + h, past, stride=2 * heads), :] for half in range(2)]
        v_cache = jnp.concatenate(halves, axis=1).astype(BF16)
        v_new = vn_ref[:, h * dv:(h + 1) * dv]
        maps = []
        for c in range(2):
            cols = slice((2 * h + c) * HEAD_DIM, (2 * h + c + 1) * HEAD_DIM)
            map_rows = pl.ds(2 * h + c, past, stride=2 * heads)
            q = q_ref[:, cols]
            s_cache = _nt_dot(q, kc_ref[map_rows, :].astype(BF16)) + bc_ref[h]
            s_new = _nt_dot(q, kn_ref[:, cols]) + bn_ref[h]
            maps.append(_two_block_softmax(s_cache, s_new, v_cache, v_new))
        o = maps[0] - lam * maps[1]
        ms = jnp.mean(o * o, axis=-1, keepdims=True)
        o_ref[:, h * dv:(h + 1) * dv] = (o * lax.rsqrt(ms + EPS) * (go_ref[...] * out_scale)).astype(BF16)


def diff_sample(q, k, v, cache_k, cache_v, layer, bias_cache, bias_new, lam, g_o, out_scale,
                row0, nb, tq):
    w = q.shape[1]
    dv = 2 * HEAD_DIM
    heads = w // dv
    past = cache_v.shape[2] // (2 * heads)
    blk0 = row0 // tq
    new = pl.BlockSpec((tq, w), lambda b: (blk0 + b, 0))
    cache = pl.BlockSpec((None, None, past * 2 * heads, HEAD_DIM), lambda b: (layer, b, 0, 0))
    return pl.pallas_call(
        functools.partial(_diff_sample_kernel, heads=heads, out_scale=out_scale),
        grid=(nb,),
        in_specs=[new, cache, cache,
                  pl.BlockSpec((heads, tq, past), lambda b: (0, 0, 0)),
                  new, new,
                  pl.BlockSpec((heads, tq, tq), lambda b: (0, 0, 0)),
                  pl.BlockSpec((1, 1), lambda b: (0, 0)),
                  pl.BlockSpec((1, dv), lambda b: (0, 0))],
        out_specs=pl.BlockSpec((tq, w), lambda b: (b, 0)),
        out_shape=jax.ShapeDtypeStruct((nb * tq, w), BF16),
        compiler_params=_params("parallel"),
        name="diff_sample",
    )(q, cache_k, cache_v, bias_cache, k, v, bias_new, lam, g_o.reshape(1, dv))


def _t5_bucket(rel):
    half = NUM_BUCKETS // 2
    max_exact = half // 2
    ret = jnp.where(rel > 0, half, 0)
    n = jnp.abs(rel)
    nf = jnp.maximum(n, 1).astype(F32)
    large = max_exact + (jnp.log(nf / max_exact) / math.log(MAX_DISTANCE / max_exact)
                         * (half - max_exact)).astype(jnp.int32)
    large = jnp.minimum(large, half - 1)
    return ret + jnp.where(n < max_exact, n, large)


def _rel_bias(table, q_pos, k_pos):
    bucket = _t5_bucket(k_pos[None, :] - q_pos[:, None])[None]
    table = table.astype(F32)
    bias = sum(jnp.where(bucket == b, table[b][:, None, None], 0.0) for b in range(NUM_BUCKETS))
    mask = (k_pos[None, :] // CHUNK) <= (q_pos[:, None] // CHUNK)
    return bias, mask


def _forward(cfg, x_prompt, x_sample, c_prompt, c_sample, cache_diff_k, cache_diff_v, cache_fox_k,
             cache_fox_v, cache_fox_logf, state_hgrn, rel_bias_table, hgrn_lb_logits, w_ada, b_ada,
             g_ffn1, w_ffn1_up, w_ffn1_down, g_mix, w_in, b_fox_f, g_hg_o, g_diff_q, g_diff_k,
             diff_lambda, g_diff_o, g_fox_q, g_fox_k, w_branch, w_out, g_ffn2, w_ffn2_up,
             w_ffn2_down):
    d, w = cfg.d_model, cfg.width
    nb_p, seq, nb_s, tq_s, past = cfg.batch, cfg.seq, cfg.dec_batch, cfg.dec_seq, cfg.past_len
    ntp, nt = cfg.nt_prompt, cfg.nt
    tm, tmb, tn, t = cfg.tm, cfg.tm_big, cfg.tn, cfg.tq
    assert tq_s == SEG and seq % t == 0 and t >= MAX_DISTANCE and t % CHUNK == 0
    fox_h, diff_h, hg_h = cfg.fox_heads, cfg.diff_heads, cfg.hg_heads

    n_seq = nb_p + nb_s
    n_seq_pad = ((n_seq + 7) // 8) * 8
    c_all = jnp.concatenate([c_prompt, c_sample, jnp.zeros((n_seq_pad - n_seq, d), F32)], axis=0)
    mod = ada_mods(c_all, w_ada, b_ada, min(cfg.t_ada, N_ADA * d))
    mod = mod.reshape(cfg.depth, n_seq_pad, N_ADA, d)
    mod_p = jnp.broadcast_to(mod[:, :nb_p, None], (cfg.depth, nb_p, seq // SEG, N_ADA, d))
    mod = jnp.concatenate([mod_p.reshape(cfg.depth, -1, N_ADA, d), mod[:, nb_p:n_seq]], axis=1)
    mod = jnp.transpose(mod, (0, 2, 1, 3))[:, :, :, None, :]

    x3 = jnp.concatenate([x_prompt.reshape(-1, SEG, d), x_sample.reshape(-1, SEG, d)], axis=0)

    cs = jnp.cumsum(jax.nn.softmax(hgrn_lb_logits.astype(F32), axis=0), axis=0)
    lbs = cs - cs[0:1]

    far = rel_bias_table[NUM_BUCKETS // 2 - 1].astype(F32)[:, None, None]
    pos_t = jnp.arange(t, dtype=jnp.int32)
    bias_d, mask_d = _rel_bias(rel_bias_table, pos_t + t, pos_t + t)
    bias_diag = jnp.swapaxes(jnp.where(mask_d[None], bias_d - far, NEG), 1, 2)
    bias_prev = jnp.swapaxes(_rel_bias(rel_bias_table, pos_t + t, pos_t)[0] - far, 1, 2)
    q_pos_s = past + jnp.arange(tq_s, dtype=jnp.int32)
    bias_s, mask_s = _rel_bias(rel_bias_table, q_pos_s, jnp.arange(past + tq_s, dtype=jnp.int32))
    bias_s = jnp.where(mask_s[None], bias_s, NEG)
    bias_s_cache, bias_s_new = bias_s[:, :, :past], bias_s[:, :, past:]

    cache_dk = cache_diff_k.reshape(cfg.depth, nb_s, past * 2 * diff_h, HEAD_DIM)
    cache_dv = cache_diff_v.reshape(cfg.depth, nb_s, past, diff_h, 2, HEAD_DIM)
    cache_dv = jnp.transpose(cache_dv, (0, 1, 2, 4, 3, 5)).reshape(cfg.depth, nb_s, -1, HEAD_DIM)
    cache_fk = cache_fox_k.reshape(cfg.depth, nb_s, past * fox_h, HEAD_DIM)
    cache_fv = cache_fox_v.reshape(cfg.depth, nb_s, past * fox_h, HEAD_DIM)
    zero_state = jnp.zeros((nb_p, hg_h, HEAD_DIM, HEAD_DIM), F32)

    diff_stacks = {"p": None, "s": None}
    fox_stacks = {"p": None, "s": None}
    states = {"p": [], "s": []}
    for li in range(cfg.depth):
        ml = mod[li]
        w_up1, w_up2 = w_ffn1_up[li].astype(BF16), w_ffn2_up[li].astype(BF16)
        wd1, wd2 = w_ffn1_down[li].astype(BF16), w_ffn2_down[li].astype(BF16)
        w_in_b = w_in[li].astype(BF16)
        b_ff = jnp.pad(b_fox_f[li].astype(F32), (0, HEAD_DIM - fox_h)).reshape(1, HEAD_DIM)
        w_br = w_branch[li].astype(BF16)
        w_o = w_out[li].astype(BF16)

        x3 = mm_residual(ffn_up(x3, g_ffn1[li], ml, 1, 0, w_up1, tmb, tn), wd1, x3, ml, 2, 0.5,
                         tmb, tn)

        h = norm_mod(x3, g_mix[li], ml, 4, 3, tm)
        lb = lbs[li].reshape(1, w)
        q_hg, g_hg, k_hg, v_hg, og = proj_hgrn(h, w_in_b, jnp.log(lb), jnp.log1p(-lb), 1.0 - lb, tm)
        groups = {}
        for name, row0, nrows in (("p", 0, ntp), ("s", ntp, nt - ntp)):
            diff_qkv, diff_stacks[name] = proj_attn(
                h, w_in_b, w, 4, g_diff_q[li], g_diff_k[li], tm, row0, nrows, li, cfg.depth,
                diff_stacks[name], head_rows=True)
            fox_qkv, fox_stacks[name] = proj_attn(
                h, w_in_b, w, 7, g_fox_q[li], g_fox_k[li], tm, row0, nrows, li, cfg.depth,
                fox_stacks[name], forget_block=10 * w // HEAD_DIM, b_forget=b_ff)
            groups[name] = diff_qkv + fox_qkv
        dq_p, dk_p, dv_p, fq_p, fk_p, fv_p = groups["p"]
        dq_s, dk_s, dv_s, fq_s, fk_s, fv_s = groups["s"]
        gates = proj_gates(h, w_in_b[:, 10 * w + fox_h:], cfg.tm_up, tn)

        o_hg_p, st_p = hgrn_mixer(q_hg, k_hg, v_hg, g_hg, og, zero_state, g_hg_o[li], 0, nb_p, seq)
        o_hg_s, st_s = hgrn_mixer(q_hg, k_hg, v_hg, g_hg, og, state_hgrn[li].astype(F32),
                                  g_hg_o[li], ntp, nb_s, tq_s)

        lam_init = 0.8 - 0.6 * math.exp(-0.3 * li)
        dl = diff_lambda[li].astype(F32)
        lam = (jnp.exp(jnp.sum(dl[0] * dl[1])) - jnp.exp(jnp.sum(dl[2] * dl[3])) + lam_init).reshape(1, 1)
        o_df_p = diff_prompt(dq_p, dk_p, jnp.transpose(dv_p), bias_diag, bias_prev, lam,
                             g_diff_o[li], 1.0 - lam_init, nb_p, seq, t, cfg.diff_group)
        o_df_s = diff_sample(dq_s, dk_s, dv_s, cache_dk, cache_dv, li, bias_s_cache, bias_s_new, lam,
                             g_diff_o[li], 1.0 - lam_init, 0, nb_s, tq_s)

        logf_p = fox_stacks["p"][2][li, :, :fox_h].reshape(nb_p, seq, fox_h)
        logf_s = fox_stacks["s"][2][li, :, :fox_h].reshape(nb_s, tq_s, fox_h)
        ck_p = jnp.broadcast_to(cumsum_time(logf_p)[..., None], (nb_p, fox_h, seq, HEAD_DIM))
        cum_s = cumsum_time(jnp.concatenate([cache_fox_logf[li].astype(F32), logf_s], axis=1))
        o_fx_p = fox_prompt(fq_p, fk_p, jnp.transpose(fv_p), ck_p, nb_p, seq, t, cfg.fox_group)
        o_fx_s = fox_sample(fq_s, fk_s, fv_s, cache_fk, cache_fv, li, cum_s[:, :, :past],
                            cum_s[:, :, past:], 0, nb_s, tq_s)

        branches = [jnp.concatenate([a, b], axis=0) for a, b in
                    ((o_hg_p, o_hg_s), (o_df_p, o_df_s), (o_fx_p, o_fx_s))]
        merged = merge_branches(branches, gates, w_br, tmb, tn)
        x3 = mm_residual(merged, w_o, x3, ml, 5, 1.0, tmb, tn)

        x3 = mm_residual(ffn_up(x3, g_ffn2[li], ml, 7, 6, w_up2, tmb, tn), wd2, x3, ml, 8, 0.5,
                         tmb, tn)

        states["p"].append(st_p)
        states["s"].append(st_s)

    def group_outputs(name, nb, tlen):
        lead = (cfg.depth, nb, tlen)
        dk, dv = diff_stacks[name]
        fk, fv, lf = fox_stacks[name]
        dv = jnp.transpose(dv.reshape(lead + (2, diff_h, HEAD_DIM)), (0, 1, 2, 4, 3, 5))
        return (dk.reshape(lead + (diff_h, 2, HEAD_DIM)), dv.reshape(lead + (diff_h, 2 * HEAD_DIM)),
                fk.reshape(lead + (fox_h, HEAD_DIM)), fv.reshape(lead + (fox_h, HEAD_DIM)),
                lf[:, :, :fox_h].reshape(lead + (fox_h,)), jnp.stack(states[name]))

    y_prompt = x3[:ntp // SEG].reshape(nb_p, seq, d)
    y_sample = x3[ntp // SEG:].reshape(nb_s, tq_s, d)
    return (y_prompt, y_sample) + group_outputs("p", nb_p, seq) + group_outputs("s", nb_s, tq_s)


def kernel(x_prompt, x_sample, c_prompt, c_sample, cache_diff_k, cache_diff_v, cache_fox_k, cache_fox_v, cache_fox_logf, state_hgrn, rel_bias_table, hgrn_lb_logits, w_ada, b_ada, g_ffn1, w_ffn1_up, w_ffn1_down, g_mix, w_in, b_fox_f, g_hg_o, g_diff_q, g_diff_k, diff_lambda, g_diff_o, g_fox_q, g_fox_k, w_branch, w_out, g_ffn2, w_ffn2_up, w_ffn2_down):
    return _forward(FULL_CFG, x_prompt, x_sample, c_prompt, c_sample, cache_diff_k, cache_diff_v,
                    cache_fox_k, cache_fox_v, cache_fox_logf, state_hgrn, rel_bias_table,
                    hgrn_lb_logits, w_ada, b_ada, g_ffn1, w_ffn1_up, w_ffn1_down, g_mix, w_in,
                    b_fox_f, g_hg_o, g_diff_q, g_diff_k, diff_lambda, g_diff_o, g_fox_q, g_fox_k,
                    w_branch, w_out, g_ffn2, w_ffn2_up, w_ffn2_down)
```

```python
import functools
import math
from typing import NamedTuple

import jax
import jax.numpy as jnp
import numpy as np
from jax import lax
from jax.experimental import pallas as pl
from jax.experimental.pallas import tpu as pltpu

F32 = jnp.float32
BF16 = jnp.bfloat16

HEAD_DIM = 128
SEG = 64
CHUNK = 64
NUM_BUCKETS = 32
MAX_DISTANCE = 128
N_ADA = 9
N_BRANCH = 3
EPS = 1e-6
NEG = -1e30
LOG2E = math.log2(math.e)
HG_CHUNK = 128
VMEM_LIMIT_BYTES = 56 * 1024 * 1024


class Cfg(NamedTuple):
    d_model: int
    batch: int
    seq: int
    depth: int
    dec_batch: int
    dec_seq: int
    past_len: int
    tm: int = 512
    tm_big: int = 1024
    tm_up: int = 2048
    tn: int = 512
    tq: int = 512
    fox_group: int = 4
    diff_group: int = 2
    t_ada: int = 1024

    @property
    def width(self):
        return self.d_model // 2

    @property
    def hg_heads(self):
        return self.width // HEAD_DIM

    @property
    def diff_heads(self):
        return self.width // (2 * HEAD_DIM)

    @property
    def fox_heads(self):
        return self.width // HEAD_DIM

    @property
    def d_ff(self):
        return ((8 * self.d_model // 3 + 127) // 128) * 128

    @property
    def d_ff_pad(self):
        return ((self.d_ff + self.tn - 1) // self.tn) * self.tn

    @property
    def nt_prompt(self):
        return self.batch * self.seq

    @property
    def nt_sample(self):
        return self.dec_batch * self.dec_seq

    @property
    def nt(self):
        return self.nt_prompt + self.nt_sample


FULL_CFG = Cfg(d_model=2048, batch=2, seq=4096, depth=4, dec_batch=32, dec_seq=64, past_len=2048)


def _params(*sem):
    return pltpu.CompilerParams(dimension_semantics=sem, vmem_limit_bytes=VMEM_LIMIT_BYTES)


def _silu(x):
    return x * jax.nn.sigmoid(x)


def _log_sigmoid(x):
    return jnp.minimum(x, 0.0) - jnp.log1p(jnp.exp(-jnp.abs(x)))


def _nt_dot(a, b):
    return lax.dot_general(a, b, (((1,), (1,)), ((), ())), preferred_element_type=F32)


def _tn_dot(a, b):
    return lax.dot_general(a, b, (((0,), (0,)), ((), ())), preferred_element_type=F32)


def _split3(x):
    x1 = x.astype(BF16)
    r1 = x - x1.astype(F32)
    x2 = r1.astype(BF16)
    x3 = (r1 - x2.astype(F32)).astype(BF16)
    return x1, x2, x3


def _exact_left_dot(m, x):
    return sum(jnp.dot(m, p, preferred_element_type=F32) for p in _split3(x))


def _ada_kernel(c_ref, w_ref, b_ref, o_ref):
    a = _silu(c_ref[...]).astype(BF16)
    o_ref[...] = jnp.dot(a, w_ref[...].astype(BF16), preferred_element_type=F32) + b_ref[...]


def ada_mods(c_all, w_ada, b_ada, tn):
    depth, d, n = w_ada.shape
    r = c_all.shape[0]
    return pl.pallas_call(
        _ada_kernel,
        grid=(depth, n // tn),
        in_specs=[pl.BlockSpec((r, d), lambda l, j: (0, 0)),
                  pl.BlockSpec((None, d, tn), lambda l, j: (l, 0, j)),
                  pl.BlockSpec((None, 1, tn), lambda l, j: (l, 0, j))],
        out_specs=pl.BlockSpec((None, r, tn), lambda l, j: (l, 0, j)),
        out_shape=jax.ShapeDtypeStruct((depth, r, n), F32),
        compiler_params=_params("parallel", "parallel"),
        name="ada_mods",
    )(c_all, w_ada, b_ada.reshape(depth, 1, n))


def _norm_mod_kernel(x_ref, g_ref, sc_ref, sh_ref, o_ref):
    x = x_ref[...]
    ms = jnp.mean(x * x, axis=-1, keepdims=True)
    y = x * lax.rsqrt(ms + EPS) * g_ref[...]
    y = y * (1.0 + sc_ref[...]) + sh_ref[...]
    o_ref[...] = y.reshape(o_ref.shape).astype(BF16)


def norm_mod(x3, g, mods, i_scale, i_shift, tm):
    ns, seg, d = x3.shape
    s = tm // seg
    return pl.pallas_call(
        _norm_mod_kernel,
        grid=(ns // s,),
        in_specs=[pl.BlockSpec((s, seg, d), lambda i: (i, 0, 0)),
                  pl.BlockSpec((1, d), lambda i: (0, 0)),
                  pl.BlockSpec((None, s, 1, d), lambda i: (i_scale, i, 0, 0)),
                  pl.BlockSpec((None, s, 1, d), lambda i: (i_shift, i, 0, 0))],
        out_specs=pl.BlockSpec((tm, d), lambda i: (i, 0)),
        out_shape=jax.ShapeDtypeStruct((ns * seg, d), BF16),
        compiler_params=_params("parallel"),
        name="norm_mod",
    )(x3, g.reshape(1, d), mods, mods)


def _ffn_up_kernel(x_ref, g_ref, sc_ref, sh_ref, wa_ref, wb_ref, o_ref, h_ref):
    @pl.when(pl.program_id(1) == 0)
    def _():
        x = x_ref[...]
        ms = jnp.mean(x * x, axis=-1, keepdims=True)
        y = x * lax.rsqrt(ms + EPS) * g_ref[...]
        y = y * (1.0 + sc_ref[...]) + sh_ref[...]
        h_ref[...] = y.reshape(h_ref.shape).astype(BF16)

    h = h_ref[...]
    a = jnp.dot(h, wa_ref[...], preferred_element_type=F32)
    b = jnp.dot(h, wb_ref[...], preferred_element_type=F32)
    o_ref[...] = (_silu(a) * b).astype(BF16)


def ffn_up(x3, g, mods, i_scale, i_shift, w_up, layer, tm, tn):
    ns, seg, d = x3.shape
    s = tm // seg
    f = w_up.shape[2] // 2
    nj = pl.cdiv(f, tn)
    wb_spec = pl.BlockSpec((None, pl.Element(d), pl.Element(tn, (0, nj * tn - f))),
                           lambda i, j: (layer, 0, pl.multiple_of(f + j * tn, HEAD_DIM)))
    return pl.pallas_call(
        _ffn_up_kernel,
        grid=(ns // s, nj),
        in_specs=[pl.BlockSpec((s, seg, d), lambda i, j: (i, 0, 0)),
                  pl.BlockSpec((1, d), lambda i, j: (0, 0)),
                  pl.BlockSpec((None, s, 1, d), lambda i, j: (i_scale, i, 0, 0)),
                  pl.BlockSpec((None, s, 1, d), lambda i, j: (i_shift, i, 0, 0)),
                  pl.BlockSpec((None, d, tn), lambda i, j: (layer, 0, j)),
                  wb_spec],
        out_specs=pl.BlockSpec((tm, tn), lambda i, j: (i, j)),
        out_shape=jax.ShapeDtypeStruct((ns * seg, f), BF16),
        scratch_shapes=[pltpu.VMEM((tm, d), BF16)],
        compiler_params=_params("parallel", "arbitrary"),
        name="ffn_up",
    )(x3, g.reshape(1, d), mods, mods, w_up, w_up)


def _mm_residual_kernel(a_ref, w_ref, x_ref, gt_ref, o_ref, *, coef):
    acc = jnp.dot(a_ref[...], w_ref[...], preferred_element_type=F32)
    o_ref[...] = x_ref[...] + (coef * gt_ref[...]) * acc.reshape(x_ref.shape)


def mm_residual(a, w, layer, x3, mods, i_gate, coef, tm, tn):
    m, k = a.shape
    n = w.shape[2]
    ns, seg, _ = x3.shape
    s = tm // seg
    return pl.pallas_call(
        functools.partial(_mm_residual_kernel, coef=coef),
        grid=(m // tm, n // tn),
        in_specs=[pl.BlockSpec((tm, k), lambda i, j: (i, 0)),
                  pl.BlockSpec((None, k, tn), lambda i, j: (layer, 0, j)),
                  pl.BlockSpec((s, seg, tn), lambda i, j: (i, 0, j)),
                  pl.BlockSpec((None, s, 1, tn), lambda i, j: (i_gate, i, 0, j))],
        out_specs=pl.BlockSpec((s, seg, tn), lambda i, j: (i, 0, j)),
        out_shape=jax.ShapeDtypeStruct(x3.shape, F32),
        compiler_params=_params("parallel", "arbitrary"),
        name="mm_residual",
    )(a, w, x3, mods)


def _proj_hgrn_kernel(h_ref, w_ref, la_ref, l1m_ref, oml_ref, q_ref, g_ref, k_ref, v_ref, og_ref):
    j = pl.program_id(1)
    acc = jnp.dot(h_ref[...], w_ref[...], preferred_element_type=F32)

    @pl.when(j == 0)
    def _():
        q_ref[...] = (_silu(acc) * (HEAD_DIM ** -0.5)).astype(BF16)

    @pl.when(j == 1)
    def _():
        la = la_ref[...]
        y = l1m_ref[...] + _log_sigmoid(acc)
        g_ref[...] = jnp.maximum(la, y) + jnp.log1p(jnp.exp(-jnp.abs(la - y)))
        k_ref[...] = (oml_ref[...] * jax.nn.sigmoid(-acc)).astype(BF16)

    @pl.when(j == 2)
    def _():
        v_ref[...] = acc.astype(BF16)

    @pl.when(j == 3)
    def _():
        og_ref[...] = jax.nn.sigmoid(acc).astype(BF16)


def proj_hgrn(h, w_main, layer, log_lb, log1m_lb, one_m_lb, tm):
    m, k = h.shape
    w = log_lb.shape[-1]
    row = lambda i, j: (i, 0)
    vec = pl.BlockSpec((1, w), lambda i, j: (0, 0))
    return pl.pallas_call(
        _proj_hgrn_kernel,
        grid=(m // tm, 4),
        in_specs=[pl.BlockSpec((tm, k), row),
                  pl.BlockSpec((None, k, w), lambda i, j: (layer, 0, j)),
                  vec, vec, vec],
        out_specs=[pl.BlockSpec((tm, w), row)] * 5,
        out_shape=[jax.ShapeDtypeStruct((m, w), BF16),
                   jax.ShapeDtypeStruct((m, w), F32),
                   jax.ShapeDtypeStruct((m, w), BF16),
                   jax.ShapeDtypeStruct((m, w), BF16),
                   jax.ShapeDtypeStruct((m, w), BF16)],
        compiler_params=_params("parallel", "arbitrary"),
        name="proj_hgrn",
    )(h, w_main, log_lb, log1m_lb, one_m_lb)


def _head_rms(acc, g):
    outs = []
    for c in range(acc.shape[1] // HEAD_DIM):
        ch = acc[:, c * HEAD_DIM:(c + 1) * HEAD_DIM]
        ms = jnp.mean(ch * ch, axis=-1, keepdims=True)
        outs.append(ch * lax.rsqrt(ms + EPS) * g)
    return outs


def _proj_attn_kernel(*refs, q_scale, with_forget, n_alias, head_rows):
    n_in = (6 if with_forget else 4) + n_alias
    h_ref, w_ref, gq_ref, gk_ref = refs[:4]
    if with_forget:
        wf_ref, bf_ref = refs[4:6]
        qb_ref, kb_ref, vb_ref, kf_ref, vf_ref, lf_ref = refs[n_in:]
    else:
        qb_ref, kb_ref, vb_ref, kf_ref, vf_ref = refs[n_in:]
    j = pl.program_id(1)
    tm = h_ref.shape[0]
    acc = jnp.dot(h_ref[...], w_ref[...], preferred_element_type=F32)
    n_chunks = acc.shape[1] // HEAD_DIM

    def chunk(c):
        return slice(c * HEAD_DIM, (c + 1) * HEAD_DIM)

    @pl.when(j == 0)
    def _():
        for c, ch in enumerate(_head_rms(acc, gq_ref[...])):
            qb_ref[:, chunk(c)] = (ch * q_scale).astype(BF16)
        if with_forget:
            f = jnp.dot(h_ref[...], wf_ref[...], preferred_element_type=F32) + bf_ref[...]
            lf_ref[...] = _log_sigmoid(f)

    @pl.when(j == 1)
    def _():
        for c, ch in enumerate(_head_rms(acc, gk_ref[...])):
            if head_rows:
                kf_ref[pl.ds(c, tm, stride=n_chunks), :] = ch
            else:
                kf_ref[:, chunk(c)] = ch
            kb_ref[:, chunk(c)] = ch.astype(BF16)

    @pl.when(j == 2)
    def _():
        if head_rows:
            for c in range(n_chunks):
                row = (c % 2) * (n_chunks // 2) + c // 2
                vf_ref[pl.ds(row, tm, stride=n_chunks), :] = acc[:, chunk(c)]
        else:
            vf_ref[...] = acc
        vb_ref[...] = acc.astype(BF16)


def proj_attn(h, w_in, w, col_block, g_q, g_k, tm, row0, nrows, layer, depth, stacks,
              head_rows=False, forget_block=None, b_forget=None):
    k = h.shape[1]
    with_forget = forget_block is not None
    blk0 = row0 // tm
    row = lambda i, j: (i, 0)
    vec = pl.BlockSpec((1, HEAD_DIM), lambda i, j: (0, 0))
    in_specs = [pl.BlockSpec((tm, k), lambda i, j: (blk0 + i, 0)),
                pl.BlockSpec((None, k, w), lambda i, j: (layer, 0, col_block + j)),
                vec, vec]
    args = [h, w_in, g_q.reshape(1, HEAD_DIM), g_k.reshape(1, HEAD_DIM)]
    if with_forget:
        in_specs += [pl.BlockSpec((None, k, HEAD_DIM), lambda i, j: (layer, 0, forget_block)), vec]
        args += [w_in, b_forget]
    n_chunks = w // HEAD_DIM
    state_shape = (nrows * n_chunks, HEAD_DIM) if head_rows else (nrows, w)
    state_block = (tm * n_chunks, HEAD_DIM) if head_rows else (tm, w)
    states = [(state_shape, state_block)] * 2
    if with_forget:
        states.append(((nrows, HEAD_DIM), (tm, HEAD_DIM)))
    out_specs = [pl.BlockSpec((tm, w), row)] * 3
    out_shape = [jax.ShapeDtypeStruct((nrows, w), BF16)] * 3
    out_specs += [pl.BlockSpec((None,) + blk, lambda i, j: (layer, i, 0)) for _, blk in states]
    out_shape += [jax.ShapeDtypeStruct((depth,) + shp, F32) for shp, _ in states]
    aliases = {}
    if stacks is not None:
        aliases = {len(args) + n: 3 + n for n in range(len(states))}
        in_specs += [pl.BlockSpec(memory_space=pl.ANY)] * len(states)
        args += list(stacks)
    res = pl.pallas_call(
        functools.partial(_proj_attn_kernel, q_scale=LOG2E * HEAD_DIM ** -0.5, with_forget=with_forget,
                          n_alias=len(aliases), head_rows=head_rows),
        grid=(nrows // tm, 3),
        in_specs=in_specs, out_specs=out_specs, out_shape=out_shape,
        input_output_aliases=aliases,
        compiler_params=_params("parallel", "arbitrary"),
        name="proj_fox" if with_forget else "proj_diff",
    )(*args)
    return res[:3], res[3:]


def _proj_gates_kernel(h_ref, w_ref, o_ref):
    acc = jnp.dot(h_ref[...], w_ref[...], preferred_element_type=F32)
    o_ref[...] = jax.nn.sigmoid(acc).astype(BF16)


def proj_gates(h, w_gates, layer, tm, tn):
    m, k = h.shape
    n = w_gates.shape[2]
    return pl.pallas_call(
        _proj_gates_kernel,
        grid=(m // tm, n // tn),
        in_specs=[pl.BlockSpec((tm, k), lambda i, j: (i, 0)),
                  pl.BlockSpec((None, k, tn), lambda i, j: (layer, 0, j))],
        out_specs=pl.BlockSpec((tm, tn), lambda i, j: (i, j)),
        out_shape=jax.ShapeDtypeStruct((m, n), BF16),
        compiler_params=_params("parallel", "arbitrary"),
        name="proj_gates",
    )(h, w_gates)


def _merge_kernel(*refs, n_prompt_tiles):
    p_refs, s_refs, g_refs = refs[0:3], refs[3:6], refs[6:9]
    w_ref, o_ref = refs[9], refs[10]

    def merge(b_refs):
        acc = None
        for n, (b_ref, g_ref) in enumerate(zip(b_refs, g_refs)):
            up = jnp.dot(b_ref[...], w_ref[n], preferred_element_type=F32)
            t = g_ref[...].astype(F32) * up
            acc = t if acc is None else acc + t
        o_ref[...] = acc.astype(BF16)

    @pl.when(pl.program_id(0) < n_prompt_tiles)
    def _():
        merge(p_refs)

    @pl.when(pl.program_id(0) >= n_prompt_tiles)
    def _():
        merge(s_refs)


def merge_branches(branches_p, branches_s, gates, w_branch, layer, tm, tn):
    mp, w = branches_p[0].shape
    ms = branches_s[0].shape[0]
    d = w_branch.shape[3]
    nj = d // tn
    npt = mp // tm
    last_p = npt - 1
    pspec = pl.BlockSpec((tm, w), lambda i, j: (jnp.minimum(i, last_p), 0))
    sspec = pl.BlockSpec((tm, w), lambda i, j: (jnp.maximum(i - npt, 0), 0))
    gspecs = [pl.BlockSpec((tm, tn), functools.partial(lambda i, j, n: (i, n * nj + j), n=n))
              for n in range(N_BRANCH)]
    return pl.pallas_call(
        functools.partial(_merge_kernel, n_prompt_tiles=npt),
        grid=((mp + ms) // tm, nj),
        in_specs=[pspec] * 3 + [sspec] * 3 + gspecs +
                 [pl.BlockSpec((None, N_BRANCH, w, tn), lambda i, j: (layer, 0, 0, j))],
        out_specs=pl.BlockSpec((tm, tn), lambda i, j: (i, j)),
        out_shape=jax.ShapeDtypeStruct((mp + ms, d), BF16),
        compiler_params=_params("parallel", "arbitrary"),
        name="merge_branches",
    )(*branches_p, *branches_s, gates, gates, gates, w_branch)


def _hgrn_level_tables(c):
    t = np.arange(c)[:, None]
    u = np.arange(c)[None, :]
    sums = [(u <= t)]
    masks = [(t == u)]
    w = 1
    while w < c:
        upper = (t // w) % 2 == 1
        q_sum = upper & (u >= (t // w) * w) & (u <= t)
        k_sum = (~upper) & (u > t) & (u <= (t // w) * w + w - 1)
        sums.append(q_sum | k_sum)
        masks.append((t // (2 * w) == u // (2 * w)) & upper & ((u // w) % 2 == 0))
        w *= 2
    return (np.stack(sums).astype(np.float32).reshape(-1, c), np.stack(masks).astype(np.float32))


def _hgrn_kernel(q_ref, k_ref, v_ref, g_ref, og_ref, s0_ref, gn_ref, sums_ref, masks_ref,
                 o_ref, sout_ref, st_ref, *, heads, rows):
    c = pl.program_id(1)
    last = pl.num_programs(1) - 1
    C = HG_CHUNK
    levels = masks_ref.shape[0] - 1

    @pl.when(c == 0)
    def _():
        for h in range(heads):
            st_ref[h] = s0_ref[h].T

    gn = gn_ref[...]

    def pad_rows(x):
        if rows == C:
            return x
        return jnp.concatenate([x, jnp.zeros((C - rows, x.shape[1]), x.dtype)], axis=0)

    qb = pad_rows(q_ref[...])
    kb = pad_rows(k_ref[...])
    v = pad_rows(v_ref[...])
    q = qb.astype(F32)
    k = kb.astype(F32)
    g1, g2, g3 = _split3(pad_rows(g_ref[...]))
    sums = (jnp.dot(sums_ref[...], g1, preferred_element_type=F32)
            + jnp.dot(sums_ref[...], g2, preferred_element_type=F32))
    b = sums[:C] + jnp.dot(sums_ref[:C, :], g3, preferred_element_type=F32)
    qe = (q * jnp.exp(b)).astype(BF16)
    b_last = b[C - 1:C, :]
    kd = (k * jnp.exp(b_last - b)).astype(BF16)
    decay = jnp.exp(b_last)
    cols = [slice(h * HEAD_DIM, (h + 1) * HEAD_DIM) for h in range(heads)]
    att = [masks_ref[0] * _nt_dot(qb[:, s], kb[:, s]) for s in cols]
    for lv in range(1, levels + 1):
        e = jnp.exp(sums[lv * C:(lv + 1) * C])
        qw = (q * e).astype(BF16)
        kw = (k * e).astype(BF16)
        mask = masks_ref[lv]
        for h, s in enumerate(cols):
            att[h] = att[h] + mask * _nt_dot(qw[:, s], kw[:, s])
    for h, s in enumerate(cols):
        st = st_ref[h]
        o = (_nt_dot(qe[:, s], st.astype(BF16))
             + jnp.dot(att[h].astype(BF16), v[:, s], preferred_element_type=F32))
        st_ref[h] = st * decay[:, s] + _tn_dot(v[:, s], kd[:, s])
        o = o[:rows]
        ms = jnp.mean(o * o, axis=-1, keepdims=True)
        o = o * lax.rsqrt(ms + EPS) * gn * og_ref[:, s].astype(F32)
        o_ref[:, s] = o.astype(BF16)

    @pl.when(c == last)
    def _():
        for h in range(heads):
            sout_ref[h] = st_ref[h].T


def hgrn_mixer(q, k, v, g, og, s0, layer, g_norm, row0, nb, t):
    w = q.shape[1]
    heads = w // HEAD_DIM
    rows = min(t, HG_CHUNK)
    nc = t // rows
    blk0 = row0 // rows
    sums, masks = _hgrn_level_tables(HG_CHUNK)
    tok = pl.BlockSpec((rows, w), lambda b, c: (blk0 + b * nc + c, 0))
    state = pl.BlockSpec((None, heads, HEAD_DIM, HEAD_DIM), lambda b, c: (b, 0, 0, 0))
    state_in = pl.BlockSpec((None, None, heads, HEAD_DIM, HEAD_DIM), lambda b, c: (layer, b, 0, 0, 0))
    return pl.pallas_call(
        functools.partial(_hgrn_kernel, heads=heads, rows=rows),
        grid=(nb, nc),
        in_specs=[tok, tok, tok, tok, tok, state_in, pl.BlockSpec((1, HEAD_DIM), lambda b, c: (0, 0)),
                  pl.BlockSpec(sums.shape, lambda b, c: (0, 0)),
                  pl.BlockSpec(masks.shape, lambda b, c: (0, 0, 0))],
        out_specs=[pl.BlockSpec((rows, w), lambda b, c: (b * nc + c, 0)), state],
        out_shape=[jax.ShapeDtypeStruct((nb * t, w), BF16),
                   jax.ShapeDtypeStruct((nb, heads, HEAD_DIM, HEAD_DIM), F32)],
        scratch_shapes=[pltpu.VMEM((heads, HEAD_DIM, HEAD_DIM), F32)],
        compiler_params=_params("parallel", "arbitrary"),
        name="hgrn_mixer",
    )(q, k, v, g, og, s0, g_norm.reshape(1, HEAD_DIM), jnp.asarray(sums, BF16), jnp.asarray(masks))


def _cumsum_kernel(x_ref, o_ref):
    rows, n, lanes = x_ref.shape
    ki = lax.broadcasted_iota(jnp.int32, (lanes, lanes), 0)
    ji = lax.broadcasted_iota(jnp.int32, (lanes, lanes), 1)
    upper = (ki <= ji).astype(BF16)
    lower = (ji < ki).astype(BF16)
    ones = jnp.ones((lanes, lanes), BF16)
    for r in range(rows):
        parts = _split3(x_ref[r])
        within = sum(jnp.dot(p, upper, preferred_element_type=F32) for p in parts)
        total = sum(jnp.dot(p, ones, preferred_element_type=F32) for p in parts)
        total = jnp.concatenate([total, jnp.zeros((lanes - n, lanes), F32)], axis=0)
        o_ref[r] = within + _exact_left_dot(lower, total)[:n]


def cumsum_time(x):
    b, t, h = x.shape
    lanes = HEAD_DIM
    tp = -(-t // (8 * lanes)) * (8 * lanes)
    n = tp // lanes
    assert n <= lanes
    xt = jnp.pad(jnp.transpose(x, (0, 2, 1)), ((0, 0), (0, 0), (0, tp - t))).reshape(b * h, n, lanes)
    rows = 8 if (b * h) % 8 == 0 else b * h
    y = pl.pallas_call(
        _cumsum_kernel,
        grid=(b * h // rows,),
        in_specs=[pl.BlockSpec((rows, n, lanes), lambda i: (i, 0, 0))],
        out_specs=pl.BlockSpec((rows, n, lanes), lambda i: (i, 0, 0)),
        out_shape=jax.ShapeDtypeStruct((b * h, n, lanes), F32),
        compiler_params=_params("parallel"),
        name="cumsum_time",
    )(xt)
    return y.reshape(b, h, tp)[:, :, :t]


def _softmax_init(m_ref, l_ref, acc_ref):
    m_ref[...] = jnp.full(m_ref.shape, NEG, F32)
    l_ref[...] = jnp.zeros(l_ref.shape, F32)
    acc_ref[...] = jnp.zeros(acc_ref.shape, F32)


def _softmax_step_t(st, vt, m_ref, l_ref, acc_ref, idx):
    m_old = m_ref[idx]
    m_new = jnp.maximum(m_old, jnp.max(st, axis=0, keepdims=True))
    alpha = jnp.exp2(m_old - m_new)
    p = jnp.exp2(st - m_new)
    l_ref[idx] = alpha * l_ref[idx] + jnp.sum(p, axis=0, keepdims=True)
    acc_ref[idx] = alpha * acc_ref[idx] + jnp.dot(vt, p.astype(BF16), preferred_element_type=F32)
    m_ref[idx] = m_new


def _fox_prompt_kernel(q_ref, k_ref, vt_ref, ck_ref, o_ref, m_ref, l_ref, acc_ref, *, t, group):
    qi = pl.program_id(2)
    _softmax_init(m_ref, l_ref, acc_ref)
    reps = t // HEAD_DIM

    def tile(kt, mask):
        rows = pl.ds(pl.multiple_of(kt * t, t), t)
        for h in range(group):
            cols = slice(h * HEAD_DIM, (h + 1) * HEAD_DIM)
            ck = ck_ref[h, rows, :]
            st = _nt_dot(k_ref[rows, cols], q_ref[:, cols]) - jnp.concatenate([ck] * reps, axis=1)
            if mask is not None:
                st = jnp.where(mask, st, NEG)
            _softmax_step_t(st, vt_ref[cols, rows], m_ref, l_ref, acc_ref, h)

    def body(kt, carry):
        tile(kt, None)
        return carry

    lax.fori_loop(0, qi, body, 0)
    key = lax.broadcasted_iota(jnp.int32, (t, t), 0)
    query = lax.broadcasted_iota(jnp.int32, (t, t), 1)
    tile(qi, key <= query)
    for h in range(group):
        o_ref[:, h * HEAD_DIM:(h + 1) * HEAD_DIM] = (acc_ref[h] / l_ref[h]).T.astype(BF16)


def fox_prompt(q, k, vt, ck, nb, seq, t, group):
    heads = q.shape[1] // HEAD_DIM
    group = min(group, heads)
    gw = group * HEAD_DIM
    nq = seq // t
    return pl.pallas_call(
        functools.partial(_fox_prompt_kernel, t=t, group=group),
        grid=(nb, heads // group, nq),
        in_specs=[pl.BlockSpec((t, gw), lambda b, h, i: (b * nq + i, h)),
                  pl.BlockSpec((seq, gw), lambda b, h, i: (b, h)),
                  pl.BlockSpec((gw, seq), lambda b, h, i: (h, b)),
                  pl.BlockSpec((None, group, seq, HEAD_DIM), lambda b, h, i: (b, h, 0, 0))],
        out_specs=pl.BlockSpec((t, gw), lambda b, h, i: (b * nq + i, h)),
        out_shape=jax.ShapeDtypeStruct((nb * seq, heads * HEAD_DIM), BF16),
        scratch_shapes=[pltpu.VMEM((group, 1, t), F32), pltpu.VMEM((group, 1, t), F32),
                        pltpu.VMEM((group, HEAD_DIM, t), F32)],
        compiler_params=_params("parallel", "parallel", "arbitrary"),
        name="fox_prompt",
    )(q, k, vt, ck)


def _diff_prompt_kernel(q_ref, k_ref, vt_ref, b0_ref, b1_ref, lam_ref, go_ref, o_ref,
                        m_ref, l_ref, acc_ref, *, t, group, out_scale):
    qi = pl.program_id(2)
    dv = 2 * HEAD_DIM
    _softmax_init(m_ref, l_ref, acc_ref)

    def tile(kt, bias_ref):
        rows = pl.ds(pl.multiple_of(kt * t, t), t)
        for h in range(group):
            vt = vt_ref[h * dv:(h + 1) * dv, rows]
            for c in range(2):
                cols = slice((2 * h + c) * HEAD_DIM, (2 * h + c + 1) * HEAD_DIM)
                st = _nt_dot(k_ref[rows, cols], q_ref[:, cols])
                if bias_ref is not None:
                    st = st + bias_ref[h]
                _softmax_step_t(st, vt, m_ref, l_ref, acc_ref, 2 * h + c)

    def body(kt, carry):
        tile(kt, None)
        return carry

    lax.fori_loop(0, jnp.maximum(qi - 1, 0), body, 0)

    @pl.when(qi >= 1)
    def _():
        tile(qi - 1, b1_ref)

    tile(qi, b0_ref)
    lam = lam_ref[...]
    for h in range(group):
        ot = acc_ref[2 * h] / l_ref[2 * h] - lam * (acc_ref[2 * h + 1] / l_ref[2 * h + 1])
        ms = jnp.mean(ot * ot, axis=0, keepdims=True)
        ot = ot * lax.rsqrt(ms + EPS) * (go_ref[...] * out_scale)
        o_ref[:, h * dv:(h + 1) * dv] = ot.T.astype(BF16)


def diff_prompt(q, k, vt, bias_diag, bias_prev, lam, g_o, out_scale, nb, seq, t, group):
    dv = 2 * HEAD_DIM
    heads = q.shape[1] // dv
    group = min(group, heads)
    gw = group * dv
    nq = seq // t
    bias = pl.BlockSpec((group, t, t), lambda b, h, i: (h, 0, 0))
    return pl.pallas_call(
        functools.partial(_diff_prompt_kernel, t=t, group=group, out_scale=out_scale),
        grid=(nb, heads // group, nq),
        in_specs=[pl.BlockSpec((t, gw), lambda b, h, i: (b * nq + i, h)),
                  pl.BlockSpec((seq, gw), lambda b, h, i: (b, h)),
                  pl.BlockSpec((gw, seq), lambda b, h, i: (h, b)),
                  bias, bias,
                  pl.BlockSpec((1, 1), lambda b, h, i: (0, 0)),
                  pl.BlockSpec((dv, 1), lambda b, h, i: (0, 0))],
        out_specs=pl.BlockSpec((t, gw), lambda b, h, i: (b * nq + i, h)),
        out_shape=jax.ShapeDtypeStruct((nb * seq, heads * dv), BF16),
        scratch_shapes=[pltpu.VMEM((2 * group, 1, t), F32), pltpu.VMEM((2 * group, 1, t), F32),
                        pltpu.VMEM((2 * group, dv, t), F32)],
        compiler_params=_params("parallel", "parallel", "arbitrary"),
        name="diff_prompt",
    )(q, k, vt, bias_diag, bias_prev, lam, g_o.reshape(dv, 1))


def _two_block_softmax(s_cache, s_new, v_cache, v_new):
    m = jnp.maximum(jnp.max(s_cache, axis=-1, keepdims=True), jnp.max(s_new, axis=-1, keepdims=True))
    p_cache = jnp.exp2(s_cache - m)
    p_new = jnp.exp2(s_new - m)
    l = jnp.sum(p_cache, axis=-1, keepdims=True) + jnp.sum(p_new, axis=-1, keepdims=True)
    o = (jnp.dot(p_cache.astype(BF16), v_cache, preferred_element_type=F32)
         + jnp.dot(p_new.astype(BF16), v_new, preferred_element_type=F32))
    return o / l


def _fox_sample_kernel(q_ref, kc_ref, vc_ref, ckc_ref, kn_ref, vn_ref, ckn_ref, o_ref, *, heads):
    tq = q_ref.shape[0]
    past = kc_ref.shape[0] // heads
    ri = lax.broadcasted_iota(jnp.int32, (tq, tq), 0)
    ci = lax.broadcasted_iota(jnp.int32, (tq, tq), 1)
    for h in range(heads):
        cols = slice(h * HEAD_DIM, (h + 1) * HEAD_DIM)
        head_rows = pl.ds(h, past, stride=heads)
        q = q_ref[:, cols]
        s_cache = _nt_dot(q, kc_ref[head_rows, :].astype(BF16)) - ckc_ref[h:h + 1, :]
        s_new = jnp.where(ci <= ri, _nt_dot(q, kn_ref[:, cols]) - ckn_ref[h:h + 1, :], NEG)
        o = _two_block_softmax(s_cache, s_new, vc_ref[head_rows, :].astype(BF16), vn_ref[:, cols])
        o_ref[:, cols] = o.astype(BF16)


def fox_sample(q, k, v, cache_k, cache_v, layer, ck_cache, ck_new, row0, nb, tq):
    w = q.shape[1]
    heads = w // HEAD_DIM
    past = cache_k.shape[2] // heads
    blk0 = row0 // tq
    new = pl.BlockSpec((tq, w), lambda b: (blk0 + b, 0))
    cache = pl.BlockSpec((None, None, past * heads, HEAD_DIM), lambda b: (layer, b, 0, 0))
    return pl.pallas_call(
        functools.partial(_fox_sample_kernel, heads=heads),
        grid=(nb,),
        in_specs=[new, cache, cache,
                  pl.BlockSpec((None, heads, past), lambda b: (b, 0, 0)),
                  new, new,
                  pl.BlockSpec((None, heads, tq), lambda b: (b, 0, 0))],
        out_specs=pl.BlockSpec((tq, w), lambda b: (b, 0)),
        out_shape=jax.ShapeDtypeStruct((nb * tq, w), BF16),
        compiler_params=_params("parallel"),
        name="fox_sample",
    )(q, cache_k, cache_v, ck_cache, k, v, ck_new)


def _diff_sample_kernel(q_ref, kc_ref, vc_ref, bc_ref, kn_ref, vn_ref, bn_ref, lam_ref, go_ref,
                        o_ref, *, heads, out_scale):
    dv = 2 * HEAD_DIM
    past = vc_ref.shape[0] // (2 * heads)
    lam = lam_ref[...]
    for h in range(heads):
        halves = [vc_ref[pl.ds(half * heads + h, past, stride=2 * heads), :] for half in range(2)]
        v_cache = jnp.concatenate(halves, axis=1).astype(BF16)
        v_new = vn_ref[:, h * dv:(h + 1) * dv]
        maps = []
        for c in range(2):
            cols = slice((2 * h + c) * HEAD_DIM, (2 * h + c + 1) * HEAD_DIM)
            map_rows = pl.ds(2 * h + c, past, stride=2 * heads)
            q = q_ref[:, cols]
            s_cache = _nt_dot(q, kc_ref[map_rows, :].astype(BF16)) + bc_ref[h]
            s_new = _nt_dot(q, kn_ref[:, cols]) + bn_ref[h]
            maps.append(_two_block_softmax(s_cache, s_new, v_cache, v_new))
        o = maps[0] - lam * maps[1]
        ms = jnp.mean(o * o, axis=-1, keepdims=True)
        o_ref[:, h * dv:(h + 1) * dv] = (o * lax.rsqrt(ms + EPS) * (go_ref[...] * out_scale)).astype(BF16)


def diff_sample(q, k, v, cache_k, cache_v, layer, bias_cache, bias_new, lam, g_o, out_scale,
                row0, nb, tq):
    w = q.shape[1]
    dv = 2 * HEAD_DIM
    heads = w // dv
    past = cache_v.shape[2] // (2 * heads)
    blk0 = row0 // tq
    new = pl.BlockSpec((tq, w), lambda b: (blk0 + b, 0))
    cache = pl.BlockSpec((None, None, past * 2 * heads, HEAD_DIM), lambda b: (layer, b, 0, 0))
    return pl.pallas_call(
        functools.partial(_diff_sample_kernel, heads=heads, out_scale=out_scale),
        grid=(nb,),
        in_specs=[new, cache, cache,
                  pl.BlockSpec((heads, tq, past), lambda b: (0, 0, 0)),
                  new, new,
                  pl.BlockSpec((heads, tq, tq), lambda b: (0, 0, 0)),
                  pl.BlockSpec((1, 1), lambda b: (0, 0)),
                  pl.BlockSpec((1, dv), lambda b: (0, 0))],
        out_specs=pl.BlockSpec((tq, w), lambda b: (b, 0)),
        out_shape=jax.ShapeDtypeStruct((nb * tq, w), BF16),
        compiler_params=_params("parallel"),
        name="diff_sample",
    )(q, cache_k, cache_v, bias_cache, k, v, bias_new, lam, g_o.reshape(1, dv))


def _t5_bucket(rel):
    half = NUM_BUCKETS // 2
    max_exact = half // 2
    ret = jnp.where(rel > 0, half, 0)
    n = jnp.abs(rel)
    nf = jnp.maximum(n, 1).astype(F32)
    large = max_exact + (jnp.log(nf / max_exact) / math.log(MAX_DISTANCE / max_exact)
                         * (half - max_exact)).astype(jnp.int32)
    large = jnp.minimum(large, half - 1)
    return ret + jnp.where(n < max_exact, n, large)


def _rel_bias(table, q_pos, k_pos):
    bucket = _t5_bucket(k_pos[None, :] - q_pos[:, None])[None]
    table = table.astype(F32)
    bias = sum(jnp.where(bucket == b, table[b][:, None, None], 0.0) for b in range(NUM_BUCKETS))
    mask = (k_pos[None, :] // CHUNK) <= (q_pos[:, None] // CHUNK)
    return bias, mask


def _forward(cfg, x_prompt, x_sample, c_prompt, c_sample, cache_diff_k, cache_diff_v, cache_fox_k,
             cache_fox_v, cache_fox_logf, state_hgrn, rel_bias_table, hgrn_lb_logits, w_ada, b_ada,
             g_ffn1, w_ffn1_up, w_ffn1_down, g_mix, w_in, b_fox_f, g_hg_o, g_diff_q, g_diff_k,
             diff_lambda, g_diff_o, g_fox_q, g_fox_k, w_branch, w_out, g_ffn2, w_ffn2_up,
             w_ffn2_down):
    d, w = cfg.d_model, cfg.width
    nb_p, seq, nb_s, tq_s, past = cfg.batch, cfg.seq, cfg.dec_batch, cfg.dec_seq, cfg.past_len
    ntp, nt = cfg.nt_prompt, cfg.nt
    tm, tmb, tn, t = cfg.tm, cfg.tm_big, cfg.tn, cfg.tq
    assert tq_s == SEG and seq % t == 0 and t >= MAX_DISTANCE and t % CHUNK == 0
    fox_h, diff_h, hg_h = cfg.fox_heads, cfg.diff_heads, cfg.hg_heads

    n_seq = nb_p + nb_s
    n_seq_pad = ((n_seq + 7) // 8) * 8
    c_all = jnp.concatenate([c_prompt, c_sample, jnp.zeros((n_seq_pad - n_seq, d), F32)], axis=0)
    mod = ada_mods(c_all, w_ada, b_ada, min(cfg.t_ada, N_ADA * d))
    mod = mod.reshape(cfg.depth, n_seq_pad, N_ADA, d)
    mod_p = jnp.broadcast_to(mod[:, :nb_p, None], (cfg.depth, nb_p, seq // SEG, N_ADA, d))
    mod = jnp.concatenate([mod_p.reshape(cfg.depth, -1, N_ADA, d), mod[:, nb_p:n_seq]], axis=1)
    mod = jnp.transpose(mod, (0, 2, 1, 3))[:, :, :, None, :]

    x3 = jnp.concatenate([x_prompt.reshape(-1, SEG, d), x_sample.reshape(-1, SEG, d)], axis=0)

    cs = jnp.cumsum(jax.nn.softmax(hgrn_lb_logits.astype(F32), axis=0), axis=0)
    lbs = cs - cs[0:1]

    far = rel_bias_table[NUM_BUCKETS // 2 - 1].astype(F32)[:, None, None]
    pos_t = jnp.arange(t, dtype=jnp.int32)
    bias_d, mask_d = _rel_bias(rel_bias_table, pos_t + t, pos_t + t)
    bias_diag = jnp.swapaxes(jnp.where(mask_d[None], LOG2E * (bias_d - far), NEG), 1, 2)
    bias_prev = jnp.swapaxes(LOG2E * (_rel_bias(rel_bias_table, pos_t + t, pos_t)[0] - far), 1, 2)
    q_pos_s = past + jnp.arange(tq_s, dtype=jnp.int32)
    bias_s, mask_s = _rel_bias(rel_bias_table, q_pos_s, jnp.arange(past + tq_s, dtype=jnp.int32))
    bias_s = jnp.where(mask_s[None], LOG2E * bias_s, NEG)
    bias_s_cache, bias_s_new = bias_s[:, :, :past], bias_s[:, :, past:]

    cache_dk = cache_diff_k.reshape(cfg.depth, nb_s, past * 2 * diff_h, HEAD_DIM)
    cache_dv = cache_diff_v.reshape(cfg.depth, nb_s, past, diff_h, 2, HEAD_DIM)
    cache_dv = jnp.transpose(cache_dv, (0, 1, 2, 4, 3, 5)).reshape(cfg.depth, nb_s, -1, HEAD_DIM)
    cache_fk = cache_fox_k.reshape(cfg.depth, nb_s, past * fox_h, HEAD_DIM)
    cache_fv = cache_fox_v.reshape(cfg.depth, nb_s, past * fox_h, HEAD_DIM)
    zero_state = jnp.zeros((1, nb_p, hg_h, HEAD_DIM, HEAD_DIM), F32)

    w_up1, w_up2 = w_ffn1_up.astype(BF16), w_ffn2_up.astype(BF16)
    wd1, wd2 = w_ffn1_down.astype(BF16), w_ffn2_down.astype(BF16)
    w_in_b = w_in.astype(BF16)
    w_gates = w_in_b[:, :, 10 * w + fox_h:]
    w_br = w_branch.astype(BF16)
    w_o = w_out.astype(BF16)

    diff_stacks = {"p": None, "s": None}
    fox_stacks = {"p": None, "s": None}
    states = {"p": [], "s": []}
    for li in range(cfg.depth):
        ml = mod[li]
        b_ff = jnp.pad(b_fox_f[li].astype(F32), (0, HEAD_DIM - fox_h)).reshape(1, HEAD_DIM)

        x3 = mm_residual(ffn_up(x3, g_ffn1[li], ml, 1, 0, w_up1, li, tmb, tn), wd1, li, x3, ml, 2,
                         0.5, tmb, tn)

        h = norm_mod(x3, g_mix[li], ml, 4, 3, tm)
        lb = lbs[li].reshape(1, w)
        q_hg, g_hg, k_hg, v_hg, og = proj_hgrn(h, w_in_b, li, jnp.log(lb), jnp.log1p(-lb), 1.0 - lb,
                                               tm)
        groups = {}
        for name, row0, nrows in (("p", 0, ntp), ("s", ntp, nt - ntp)):
            diff_qkv, diff_stacks[name] = proj_attn(
                h, w_in_b, w, 4, g_diff_q[li], g_diff_k[li], tm, row0, nrows, li, cfg.depth,
                diff_stacks[name], head_rows=True)
            fox_qkv, fox_stacks[name] = proj_attn(
                h, w_in_b, w, 7, g_fox_q[li], g_fox_k[li], tm, row0, nrows, li, cfg.depth,
                fox_stacks[name], forget_block=10 * w // HEAD_DIM, b_forget=b_ff)
            groups[name] = diff_qkv + fox_qkv
        dq_p, dk_p, dv_p, fq_p, fk_p, fv_p = groups["p"]
        dq_s, dk_s, dv_s, fq_s, fk_s, fv_s = groups["s"]
        gates = proj_gates(h, w_gates, li, cfg.tm_up, tn)

        o_hg_p, st_p = hgrn_mixer(q_hg, k_hg, v_hg, g_hg, og, zero_state, 0, g_hg_o[li], 0, nb_p, seq)
        o_hg_s, st_s = hgrn_mixer(q_hg, k_hg, v_hg, g_hg, og, state_hgrn.astype(F32), li,
                                  g_hg_o[li], ntp, nb_s, tq_s)

        lam_init = 0.8 - 0.6 * math.exp(-0.3 * li)
        dl = diff_lambda[li].astype(F32)
        lam = (jnp.exp(jnp.sum(dl[0] * dl[1])) - jnp.exp(jnp.sum(dl[2] * dl[3])) + lam_init).reshape(1, 1)
        o_df_p = diff_prompt(dq_p, dk_p, jnp.transpose(dv_p), bias_diag, bias_prev, lam,
                             g_diff_o[li], 1.0 - lam_init, nb_p, seq, t, cfg.diff_group)
        o_df_s = diff_sample(dq_s, dk_s, dv_s, cache_dk, cache_dv, li, bias_s_cache, bias_s_new, lam,
                             g_diff_o[li], 1.0 - lam_init, 0, nb_s, tq_s)

        logf_p = fox_stacks["p"][2][li, :, :fox_h].reshape(nb_p, seq, fox_h)
        logf_s = fox_stacks["s"][2][li, :, :fox_h].reshape(nb_s, tq_s, fox_h)
        ck_p = jnp.broadcast_to((LOG2E * cumsum_time(logf_p))[..., None],
                                (nb_p, fox_h, seq, HEAD_DIM))
        cum_s = LOG2E * cumsum_time(
            jnp.concatenate([cache_fox_logf[li].astype(F32), logf_s], axis=1))
        o_fx_p = fox_prompt(fq_p, fk_p, jnp.transpose(fv_p), ck_p, nb_p, seq, t, cfg.fox_group)
        o_fx_s = fox_sample(fq_s, fk_s, fv_s, cache_fk, cache_fv, li, cum_s[:, :, :past],
                            cum_s[:, :, past:], 0, nb_s, tq_s)

        merged = merge_branches((o_hg_p, o_df_p, o_fx_p), (o_hg_s, o_df_s, o_fx_s), gates, w_br, li,
                                tmb, tn)
        x3 = mm_residual(merged, w_o, li, x3, ml, 5, 1.0, tmb, tn)

        x3 = mm_residual(ffn_up(x3, g_ffn2[li], ml, 7, 6, w_up2, li, tmb, tn), wd2, li, x3, ml, 8,
                         0.5, tmb, tn)

        states["p"].append(st_p)
        states["s"].append(st_s)

    def group_outputs(name, nb, tlen):
        lead = (cfg.depth, nb, tlen)
        dk, dv = diff_stacks[name]
        fk, fv, lf = fox_stacks[name]
        dv = jnp.transpose(dv.reshape(lead + (2, diff_h, HEAD_DIM)), (0, 1, 2, 4, 3, 5))
        return (dk.reshape(lead + (diff_h, 2, HEAD_DIM)), dv.reshape(lead + (diff_h, 2 * HEAD_DIM)),
                fk.reshape(lead + (fox_h, HEAD_DIM)), fv.reshape(lead + (fox_h, HEAD_DIM)),
                lf[:, :, :fox_h].reshape(lead + (fox_h,)), jnp.stack(states[name]))

    y_prompt = x3[:ntp // SEG].reshape(nb_p, seq, d)
    y_sample = x3[ntp // SEG:].reshape(nb_s, tq_s, d)
    return (y_prompt, y_sample) + group_outputs("p", nb_p, seq) + group_outputs("s", nb_s, tq_s)


def kernel(x_prompt, x_sample, c_prompt, c_sample, cache_diff_k, cache_diff_v, cache_fox_k, cache_fox_v, cache_fox_logf, state_hgrn, rel_bias_table, hgrn_lb_logits, w_ada, b_ada, g_ffn1, w_ffn1_up, w_ffn1_down, g_mix, w_in, b_fox_f, g_hg_o, g_diff_q, g_diff_k, diff_lambda, g_diff_o, g_fox_q, g_fox_k, w_branch, w_out, g_ffn2, w_ffn2_up, w_ffn2_down):
    return _forward(FULL_CFG, x_prompt, x_sample, c_prompt, c_sample, cache_diff_k, cache_diff_v,
                    cache_fox_k, cache_fox_v, cache_fox_logf, state_hgrn, rel_bias_table,
                    hgrn_lb_logits, w_ada, b_ada, g_ffn1, w_ffn1_up, w_ffn1_down, g_mix, w_in,
                    b_fox_f, g_hg_o, g_diff_q, g_diff_k, diff_lambda, g_diff_o, g_fox_q, g_fox_k,
                    w_branch, w_out, g_ffn2, w_ffn2_up, w_ffn2_down)
```

```python
import functools
import math
from typing import NamedTuple

import jax
import jax.numpy as jnp
import numpy as np
from jax import lax
from jax.experimental import pallas as pl
from jax.experimental.pallas import tpu as pltpu

F32 = jnp.float32
BF16 = jnp.bfloat16

HEAD_DIM = 128
SEG = 64
CHUNK = 64
NUM_BUCKETS = 32
MAX_DISTANCE = 128
N_ADA = 9
N_BRANCH = 3
EPS = 1e-6
NEG = -1e30
LOG2E = math.log2(math.e)
HG_CHUNK = 128
SUBLANES = 8
VMEM_LIMIT_BYTES = 56 * 1024 * 1024


class Cfg(NamedTuple):
    d_model: int
    batch: int
    seq: int
    depth: int
    dec_batch: int
    dec_seq: int
    past_len: int
    tm: int = 1024
    tm_big: int = 1024
    tm_up: int = 2048
    tn: int = 512
    tq: int = 512
    fox_group: int = 4
    diff_group: int = 2
    t_ada: int = 1024

    @property
    def width(self):
        return self.d_model // 2

    @property
    def hg_heads(self):
        return self.width // HEAD_DIM

    @property
    def diff_heads(self):
        return self.width // (2 * HEAD_DIM)

    @property
    def fox_heads(self):
        return self.width // HEAD_DIM

    @property
    def d_ff(self):
        return ((8 * self.d_model // 3 + 127) // 128) * 128

    @property
    def d_ff_pad(self):
        return ((self.d_ff + self.tn - 1) // self.tn) * self.tn

    @property
    def nt_prompt(self):
        return self.batch * self.seq

    @property
    def nt_sample(self):
        return self.dec_batch * self.dec_seq

    @property
    def nt(self):
        return self.nt_prompt + self.nt_sample


FULL_CFG = Cfg(d_model=2048, batch=2, seq=4096, depth=4, dec_batch=32, dec_seq=64, past_len=2048)


def _params(*sem):
    return pltpu.CompilerParams(dimension_semantics=sem, vmem_limit_bytes=VMEM_LIMIT_BYTES)


def _silu(x):
    return x * jax.nn.sigmoid(x)


def _log_sigmoid(x):
    return jnp.minimum(x, 0.0) - jnp.log1p(jnp.exp(-jnp.abs(x)))


def _nt_dot(a, b):
    return lax.dot_general(a, b, (((1,), (1,)), ((), ())), preferred_element_type=F32)


def _tn_dot(a, b):
    return lax.dot_general(a, b, (((0,), (0,)), ((), ())), preferred_element_type=F32)


def _split3(x):
    x1 = x.astype(BF16)
    r1 = x - x1.astype(F32)
    x2 = r1.astype(BF16)
    x3 = (r1 - x2.astype(F32)).astype(BF16)
    return x1, x2, x3


def _exact_left_dot(m, x):
    return sum(jnp.dot(m, p, preferred_element_type=F32) for p in _split3(x))


def _ada_kernel(c_ref, w_ref, b_ref, o_ref):
    a = _silu(c_ref[...]).astype(BF16)
    o_ref[...] = jnp.dot(a, w_ref[...].astype(BF16), preferred_element_type=F32) + b_ref[...]


def ada_mods(c_all, w_ada, b_ada, tn):
    depth, d, n = w_ada.shape
    r = c_all.shape[0]
    return pl.pallas_call(
        _ada_kernel,
        grid=(depth, n // tn),
        in_specs=[pl.BlockSpec((r, d), lambda l, j: (0, 0)),
                  pl.BlockSpec((None, d, tn), lambda l, j: (l, 0, j)),
                  pl.BlockSpec((None, 1, tn), lambda l, j: (l, 0, j))],
        out_specs=pl.BlockSpec((None, r, tn), lambda l, j: (l, 0, j)),
        out_shape=jax.ShapeDtypeStruct((depth, r, n), F32),
        compiler_params=_params("parallel", "parallel"),
        name="ada_mods",
    )(c_all, w_ada, b_ada.reshape(depth, 1, n))


def _norm_mod_kernel(x_ref, g_ref, sc_ref, sh_ref, o_ref):
    x = x_ref[...]
    ms = jnp.mean(x * x, axis=-1, keepdims=True)
    y = x * lax.rsqrt(ms + EPS) * g_ref[...]
    y = y * (1.0 + sc_ref[...]) + sh_ref[...]
    o_ref[...] = y.reshape(o_ref.shape).astype(BF16)


def norm_mod(x3, g, mods, i_scale, i_shift, tm):
    ns, seg, d = x3.shape
    s = tm // seg
    return pl.pallas_call(
        _norm_mod_kernel,
        grid=(ns // s,),
        in_specs=[pl.BlockSpec((s, seg, d), lambda i: (i, 0, 0)),
                  pl.BlockSpec((1, d), lambda i: (0, 0)),
                  pl.BlockSpec((None, s, 1, d), lambda i: (i_scale, i, 0, 0)),
                  pl.BlockSpec((None, s, 1, d), lambda i: (i_shift, i, 0, 0))],
        out_specs=pl.BlockSpec((tm, d), lambda i: (i, 0)),
        out_shape=jax.ShapeDtypeStruct((ns * seg, d), BF16),
        compiler_params=_params("parallel"),
        name="norm_mod",
    )(x3, g.reshape(1, d), mods, mods)


def _ffn_up_kernel(x_ref, g_ref, sc_ref, sh_ref, wa_ref, wb_ref, o_ref, h_ref):
    @pl.when(pl.program_id(1) == 0)
    def _():
        x = x_ref[...]
        ms = jnp.mean(x * x, axis=-1, keepdims=True)
        y = x * lax.rsqrt(ms + EPS) * g_ref[...]
        y = y * (1.0 + sc_ref[...]) + sh_ref[...]
        h_ref[...] = y.reshape(h_ref.shape).astype(BF16)

    h = h_ref[...]
    a = jnp.dot(h, wa_ref[...], preferred_element_type=F32)
    b = jnp.dot(h, wb_ref[...], preferred_element_type=F32)
    o_ref[...] = (_silu(a) * b).astype(BF16)


def ffn_up(x3, g, mods, i_scale, i_shift, w_up, layer, tm, tn):
    ns, seg, d = x3.shape
    s = tm // seg
    f = w_up.shape[2] // 2
    nj = pl.cdiv(f, tn)
    wb_spec = pl.BlockSpec((None, pl.Element(d), pl.Element(tn, (0, nj * tn - f))),
                           lambda i, j: (layer, 0, pl.multiple_of(f + j * tn, HEAD_DIM)))
    return pl.pallas_call(
        _ffn_up_kernel,
        grid=(ns // s, nj),
        in_specs=[pl.BlockSpec((s, seg, d), lambda i, j: (i, 0, 0)),
                  pl.BlockSpec((1, d), lambda i, j: (0, 0)),
                  pl.BlockSpec((None, s, 1, d), lambda i, j: (i_scale, i, 0, 0)),
                  pl.BlockSpec((None, s, 1, d), lambda i, j: (i_shift, i, 0, 0)),
                  pl.BlockSpec((None, d, tn), lambda i, j: (layer, 0, j)),
                  wb_spec],
        out_specs=pl.BlockSpec((tm, tn), lambda i, j: (i, j)),
        out_shape=jax.ShapeDtypeStruct((ns * seg, f), BF16),
        scratch_shapes=[pltpu.VMEM((tm, d), BF16)],
        compiler_params=_params("parallel", "arbitrary"),
        name="ffn_up",
    )(x3, g.reshape(1, d), mods, mods, w_up, w_up)


def _mm_residual_kernel(a_ref, w_ref, x_ref, gt_ref, o_ref, *, coef):
    acc = jnp.dot(a_ref[...], w_ref[...], preferred_element_type=F32)
    o_ref[...] = x_ref[...] + (coef * gt_ref[...]) * acc.reshape(x_ref.shape)


def mm_residual(a, w, layer, x3, mods, i_gate, coef, tm, tn):
    m, k = a.shape
    n = w.shape[2]
    ns, seg, _ = x3.shape
    s = tm // seg
    return pl.pallas_call(
        functools.partial(_mm_residual_kernel, coef=coef),
        grid=(m // tm, n // tn),
        in_specs=[pl.BlockSpec((tm, k), lambda i, j: (i, 0)),
                  pl.BlockSpec((None, k, tn), lambda i, j: (layer, 0, j)),
                  pl.BlockSpec((s, seg, tn), lambda i, j: (i, 0, j)),
                  pl.BlockSpec((None, s, 1, tn), lambda i, j: (i_gate, i, 0, j))],
        out_specs=pl.BlockSpec((s, seg, tn), lambda i, j: (i, 0, j)),
        out_shape=jax.ShapeDtypeStruct(x3.shape, F32),
        compiler_params=_params("parallel", "arbitrary"),
        name="mm_residual",
    )(a, w, x3, mods)


def _proj_hgrn_kernel(h_ref, w_ref, la_ref, l1m_ref, oml_ref, q_ref, g_ref, k_ref, v_ref, og_ref):
    j = pl.program_id(1)
    acc = jnp.dot(h_ref[...], w_ref[...], preferred_element_type=F32)

    @pl.when(j == 0)
    def _():
        q_ref[...] = (_silu(acc) * (HEAD_DIM ** -0.5)).astype(BF16)

    @pl.when(j == 1)
    def _():
        la = la_ref[...]
        y = l1m_ref[...] + _log_sigmoid(acc)
        g_ref[...] = jnp.maximum(la, y) + jnp.log1p(jnp.exp(-jnp.abs(la - y)))
        k_ref[...] = (oml_ref[...] * jax.nn.sigmoid(-acc)).astype(BF16)

    @pl.when(j == 2)
    def _():
        v_ref[...] = acc.astype(BF16)

    @pl.when(j == 3)
    def _():
        og_ref[...] = jax.nn.sigmoid(acc).astype(BF16)


def proj_hgrn(h, w_main, layer, log_lb, log1m_lb, one_m_lb, tm):
    m, k = h.shape
    w = log_lb.shape[-1]
    row = lambda i, j: (i, 0)
    vec = pl.BlockSpec((1, w), lambda i, j: (0, 0))
    return pl.pallas_call(
        _proj_hgrn_kernel,
        grid=(m // tm, 4),
        in_specs=[pl.BlockSpec((tm, k), row),
                  pl.BlockSpec((None, k, w), lambda i, j: (layer, 0, j)),
                  vec, vec, vec],
        out_specs=[pl.BlockSpec((tm, w), row)] * 5,
        out_shape=[jax.ShapeDtypeStruct((m, w), BF16),
                   jax.ShapeDtypeStruct((m, w), F32),
                   jax.ShapeDtypeStruct((m, w), BF16),
                   jax.ShapeDtypeStruct((m, w), BF16),
                   jax.ShapeDtypeStruct((m, w), BF16)],
        compiler_params=_params("parallel", "arbitrary"),
        name="proj_hgrn",
    )(h, w_main, log_lb, log1m_lb, one_m_lb)


def _head_rms(acc, g):
    outs = []
    for c in range(acc.shape[1] // HEAD_DIM):
        ch = acc[:, c * HEAD_DIM:(c + 1) * HEAD_DIM]
        ms = jnp.mean(ch * ch, axis=-1, keepdims=True)
        outs.append(ch * lax.rsqrt(ms + EPS) * g)
    return outs


def _proj_attn_kernel(*refs, q_scale, with_forget, n_alias, head_rows):
    n_in = (6 if with_forget else 4) + n_alias
    h_ref, w_ref, gq_ref, gk_ref = refs[:4]
    if with_forget:
        wf_ref, bf_ref = refs[4:6]
        qb_ref, kb_ref, vb_ref, kf_ref, vf_ref, lf_ref = refs[n_in:]
    else:
        qb_ref, kb_ref, vb_ref, kf_ref, vf_ref = refs[n_in:]
    j = pl.program_id(1)
    tm = h_ref.shape[0]
    acc = jnp.dot(h_ref[...], w_ref[...], preferred_element_type=F32)
    n_chunks = acc.shape[1] // HEAD_DIM

    def chunk(c):
        return slice(c * HEAD_DIM, (c + 1) * HEAD_DIM)

    @pl.when(j == 0)
    def _():
        for c, ch in enumerate(_head_rms(acc, gq_ref[...])):
            qb_ref[:, chunk(c)] = (ch * q_scale).astype(BF16)
        if with_forget:
            f = jnp.dot(h_ref[...], wf_ref[...], preferred_element_type=F32) + bf_ref[...]
            lf_ref[...] = _log_sigmoid(f)

    @pl.when(j == 1)
    def _():
        for c, ch in enumerate(_head_rms(acc, gk_ref[...])):
            if head_rows:
                kf_ref[pl.ds(c, tm, stride=n_chunks), :] = ch
            else:
                kf_ref[:, chunk(c)] = ch
            kb_ref[:, chunk(c)] = ch.astype(BF16)

    @pl.when(j == 2)
    def _():
        if head_rows:
            for c in range(n_chunks):
                row = (c % 2) * (n_chunks // 2) + c // 2
                vf_ref[pl.ds(row, tm, stride=n_chunks), :] = acc[:, chunk(c)]
        else:
            vf_ref[...] = acc
        vb_ref[...] = acc.astype(BF16)


def proj_attn(h, w_in, w, col_block, g_q, g_k, tm, row0, nrows, layer, depth, stacks,
              head_rows=False, forget_block=None, b_forget=None):
    k = h.shape[1]
    with_forget = forget_block is not None
    blk0 = row0 // tm
    row = lambda i, j: (i, 0)
    vec = pl.BlockSpec((1, HEAD_DIM), lambda i, j: (0, 0))
    in_specs = [pl.BlockSpec((tm, k), lambda i, j: (blk0 + i, 0)),
                pl.BlockSpec((None, k, w), lambda i, j: (layer, 0, col_block + j)),
                vec, vec]
    args = [h, w_in, g_q.reshape(1, HEAD_DIM), g_k.reshape(1, HEAD_DIM)]
    if with_forget:
        in_specs += [pl.BlockSpec((None, k, HEAD_DIM), lambda i, j: (layer, 0, forget_block)), vec]
        args += [w_in, b_forget]
    n_chunks = w // HEAD_DIM
    state_shape = (nrows * n_chunks, HEAD_DIM) if head_rows else (nrows, w)
    state_block = (tm * n_chunks, HEAD_DIM) if head_rows else (tm, w)
    states = [(state_shape, state_block)] * 2
    if with_forget:
        states.append(((nrows, HEAD_DIM), (tm, HEAD_DIM)))
    out_specs = [pl.BlockSpec((tm, w), row)] * 3
    out_shape = [jax.ShapeDtypeStruct((nrows, w), BF16)] * 3
    out_specs += [pl.BlockSpec((None,) + blk, lambda i, j: (layer, i, 0)) for _, blk in states]
    out_shape += [jax.ShapeDtypeStruct((depth,) + shp, F32) for shp, _ in states]
    aliases = {}
    if stacks is not None:
        aliases = {len(args) + n: 3 + n for n in range(len(states))}
        in_specs += [pl.BlockSpec(memory_space=pl.ANY)] * len(states)
        args += list(stacks)
    res = pl.pallas_call(
        functools.partial(_proj_attn_kernel, q_scale=LOG2E * HEAD_DIM ** -0.5, with_forget=with_forget,
                          n_alias=len(aliases), head_rows=head_rows),
        grid=(nrows // tm, 3),
        in_specs=in_specs, out_specs=out_specs, out_shape=out_shape,
        input_output_aliases=aliases,
        compiler_params=_params("parallel", "arbitrary"),
        name="proj_fox" if with_forget else "proj_diff",
    )(*args)
    return res[:3], res[3:]


def _proj_gates_kernel(h_ref, w_ref, o_ref):
    acc = jnp.dot(h_ref[...], w_ref[...], preferred_element_type=F32)
    o_ref[...] = jax.nn.sigmoid(acc).astype(BF16)


def proj_gates(h, w_gates, layer, tm, tn):
    m, k = h.shape
    n = w_gates.shape[2]
    return pl.pallas_call(
        _proj_gates_kernel,
        grid=(m // tm, n // tn),
        in_specs=[pl.BlockSpec((tm, k), lambda i, j: (i, 0)),
                  pl.BlockSpec((None, k, tn), lambda i, j: (layer, 0, j))],
        out_specs=pl.BlockSpec((tm, tn), lambda i, j: (i, j)),
        out_shape=jax.ShapeDtypeStruct((m, n), BF16),
        compiler_params=_params("parallel", "arbitrary"),
        name="proj_gates",
    )(h, w_gates)


def _merge_kernel(*refs, n_prompt_tiles):
    p_refs, s_refs, g_refs = refs[0:3], refs[3:6], refs[6:9]
    w_ref, o_ref = refs[9], refs[10]

    def merge(b_refs):
        acc = None
        for n, (b_ref, g_ref) in enumerate(zip(b_refs, g_refs)):
            up = jnp.dot(b_ref[...], w_ref[n], preferred_element_type=F32)
            t = g_ref[...].astype(F32) * up
            acc = t if acc is None else acc + t
        o_ref[...] = acc.astype(BF16)

    @pl.when(pl.program_id(0) < n_prompt_tiles)
    def _():
        merge(p_refs)

    @pl.when(pl.program_id(0) >= n_prompt_tiles)
    def _():
        merge(s_refs)


def merge_branches(branches_p, branches_s, gates, w_branch, layer, tm, tn):
    mp, w = branches_p[0].shape
    ms = branches_s[0].shape[0]
    d = w_branch.shape[3]
    nj = d // tn
    npt = mp // tm
    last_p = npt - 1
    pspec = pl.BlockSpec((tm, w), lambda i, j: (jnp.minimum(i, last_p), 0))
    sspec = pl.BlockSpec((tm, w), lambda i, j: (jnp.maximum(i - npt, 0), 0))
    gspecs = [pl.BlockSpec((tm, tn), functools.partial(lambda i, j, n: (i, n * nj + j), n=n))
              for n in range(N_BRANCH)]
    return pl.pallas_call(
        functools.partial(_merge_kernel, n_prompt_tiles=npt),
        grid=((mp + ms) // tm, nj),
        in_specs=[pspec] * 3 + [sspec] * 3 + gspecs +
                 [pl.BlockSpec((None, N_BRANCH, w, tn), lambda i, j: (layer, 0, 0, j))],
        out_specs=pl.BlockSpec((tm, tn), lambda i, j: (i, j)),
        out_shape=jax.ShapeDtypeStruct((mp + ms, d), BF16),
        compiler_params=_params("parallel", "arbitrary"),
        name="merge_branches",
    )(*branches_p, *branches_s, gates, gates, gates, w_branch)


def _hgrn_level_tables(c):
    t = np.arange(c)[:, None]
    u = np.arange(c)[None, :]
    sums = [(u <= t)]
    masks = [(t == u)]
    w = 1
    while w < c:
        upper = (t // w) % 2 == 1
        if 2 * w < SUBLANES:
            q_sum = upper & (u >= (t // w) * w) & (u <= t)
            k_sum = (~upper) & (u > t) & (u <= (t // w) * w + w - 1)
            sums.append(q_sum | k_sum)
        masks.append((t // (2 * w) == u // (2 * w)) & upper & ((u // w) % 2 == 0))
        w *= 2
    return (np.stack(sums).astype(np.float32).reshape(-1, c), np.stack(masks).astype(np.float32))


def _hgrn_kernel(q_ref, k_ref, v_ref, g_ref, og_ref, s0_ref, gn_ref, sums_ref, masks_ref,
                 o_ref, sout_ref, st_ref, *, heads, rows):
    c = pl.program_id(1)
    last = pl.num_programs(1) - 1
    C = HG_CHUNK
    levels = masks_ref.shape[0] - 1

    @pl.when(c == 0)
    def _():
        for h in range(heads):
            st_ref[h] = s0_ref[h].T

    gn = gn_ref[...]

    def pad_rows(x):
        if rows == C:
            return x
        return jnp.concatenate([x, jnp.zeros((C - rows, x.shape[1]), x.dtype)], axis=0)

    qb = pad_rows(q_ref[...])
    kb = pad_rows(k_ref[...])
    v = pad_rows(v_ref[...])
    q = qb.astype(F32)
    k = kb.astype(F32)
    g1, g2, g3 = _split3(pad_rows(g_ref[...]))
    sums = (jnp.dot(sums_ref[...], g1, preferred_element_type=F32)
            + jnp.dot(sums_ref[...], g2, preferred_element_type=F32))
    b = sums[:C] + jnp.dot(sums_ref[:C, :], g3, preferred_element_type=F32)
    qe = (q * jnp.exp(b)).astype(BF16)
    b_last = b[C - 1:C, :]
    kd = (k * jnp.exp(b_last - b)).astype(BF16)
    decay = jnp.exp(b_last)
    cols = [slice(h * HEAD_DIM, (h + 1) * HEAD_DIM) for h in range(heads)]
    att = [masks_ref[0] * _nt_dot(qb[:, s], kb[:, s]) for s in cols]
    n_table = sums_ref.shape[0] // C - 1
    row = lax.broadcasted_iota(jnp.int32, b.shape, 0)
    for lv in range(1, levels + 1):
        half = 1 << (lv - 1)
        if lv <= n_table:
            part = sums[lv * C:(lv + 1) * C]
        else:
            mid = [jnp.broadcast_to(b[blk + half - 1:blk + half, :], (2 * half, b.shape[1]))
                   for blk in range(0, C, 2 * half)]
            mid = mid[0] if len(mid) == 1 else jnp.concatenate(mid, axis=0)
            upper = ((row >> (lv - 1)) & 1) == 1
            part = jnp.where(upper, b - mid, mid - b)
        e = jnp.exp(part)
        qw = (q * e).astype(BF16)
        kw = (k * e).astype(BF16)
        mask = masks_ref[lv]
        for h, s in enumerate(cols):
            att[h] = att[h] + mask * _nt_dot(qw[:, s], kw[:, s])
    for h, s in enumerate(cols):
        st = st_ref[h]
        o = (_nt_dot(qe[:, s], st.astype(BF16))
             + jnp.dot(att[h].astype(BF16), v[:, s], preferred_element_type=F32))
        st_ref[h] = st * decay[:, s] + _tn_dot(v[:, s], kd[:, s])
        o = o[:rows]
        ms = jnp.mean(o * o, axis=-1, keepdims=True)
        o = o * lax.rsqrt(ms + EPS) * gn * og_ref[:, s].astype(F32)
        o_ref[:, s] = o.astype(BF16)

    @pl.when(c == last)
    def _():
        for h in range(heads):
            sout_ref[h] = st_ref[h].T


def hgrn_mixer(q, k, v, g, og, s0, layer, g_norm, row0, nb, t):
    w = q.shape[1]
    heads = w // HEAD_DIM
    rows = min(t, HG_CHUNK)
    nc = t // rows
    blk0 = row0 // rows
    sums, masks = _hgrn_level_tables(HG_CHUNK)
    tok = pl.BlockSpec((rows, w), lambda b, c: (blk0 + b * nc + c, 0))
    state = pl.BlockSpec((None, heads, HEAD_DIM, HEAD_DIM), lambda b, c: (b, 0, 0, 0))
    state_in = pl.BlockSpec((None, None, heads, HEAD_DIM, HEAD_DIM), lambda b, c: (layer, b, 0, 0, 0))
    return pl.pallas_call(
        functools.partial(_hgrn_kernel, heads=heads, rows=rows),
        grid=(nb, nc),
        in_specs=[tok, tok, tok, tok, tok, state_in, pl.BlockSpec((1, HEAD_DIM), lambda b, c: (0, 0)),
                  pl.BlockSpec(sums.shape, lambda b, c: (0, 0)),
                  pl.BlockSpec(masks.shape, lambda b, c: (0, 0, 0))],
        out_specs=[pl.BlockSpec((rows, w), lambda b, c: (b * nc + c, 0)), state],
        out_shape=[jax.ShapeDtypeStruct((nb * t, w), BF16),
                   jax.ShapeDtypeStruct((nb, heads, HEAD_DIM, HEAD_DIM), F32)],
        scratch_shapes=[pltpu.VMEM((heads, HEAD_DIM, HEAD_DIM), F32)],
        compiler_params=_params("parallel", "arbitrary"),
        name="hgrn_mixer",
    )(q, k, v, g, og, s0, g_norm.reshape(1, HEAD_DIM), jnp.asarray(sums, BF16), jnp.asarray(masks))


def _cumsum_kernel(x_ref, o_ref):
    rows, n, lanes = x_ref.shape
    ki = lax.broadcasted_iota(jnp.int32, (lanes, lanes), 0)
    ji = lax.broadcasted_iota(jnp.int32, (lanes, lanes), 1)
    upper = (ki <= ji).astype(BF16)
    lower = (ji < ki).astype(BF16)
    ones = jnp.ones((lanes, lanes), BF16)
    for r in range(rows):
        parts = _split3(x_ref[r])
        within = sum(jnp.dot(p, upper, preferred_element_type=F32) for p in parts)
        total = sum(jnp.dot(p, ones, preferred_element_type=F32) for p in parts)
        total = jnp.concatenate([total, jnp.zeros((lanes - n, lanes), F32)], axis=0)
        o_ref[r] = within + _exact_left_dot(lower, total)[:n]


def cumsum_time(x):
    b, t, h = x.shape
    lanes = HEAD_DIM
    tp = -(-t // (8 * lanes)) * (8 * lanes)
    n = tp // lanes
    assert n <= lanes
    xt = jnp.pad(jnp.transpose(x, (0, 2, 1)), ((0, 0), (0, 0), (0, tp - t))).reshape(b * h, n, lanes)
    rows = 8 if (b * h) % 8 == 0 else b * h
    y = pl.pallas_call(
        _cumsum_kernel,
        grid=(b * h // rows,),
        in_specs=[pl.BlockSpec((rows, n, lanes), lambda i: (i, 0, 0))],
        out_specs=pl.BlockSpec((rows, n, lanes), lambda i: (i, 0, 0)),
        out_shape=jax.ShapeDtypeStruct((b * h, n, lanes), F32),
        compiler_params=_params("parallel"),
        name="cumsum_time",
    )(xt)
    return y.reshape(b, h, tp)[:, :, :t]


def _softmax_init(m_ref, l_ref, acc_ref):
    m_ref[...] = jnp.full(m_ref.shape, NEG, F32)
    l_ref[...] = jnp.zeros(l_ref.shape, F32)
    acc_ref[...] = jnp.zeros(acc_ref.shape, F32)


QUERY_SEG = 256


def _softmax_step_t(st, vt, m_ref, l_ref, acc_ref, idx, seg=slice(None)):
    m_old = m_ref[idx, :, seg]
    m_new = jnp.maximum(m_old, jnp.max(st, axis=0, keepdims=True))
    alpha = jnp.exp2(m_old - m_new)
    p = jnp.exp2(st - m_new)
    l_ref[idx, :, seg] = alpha * l_ref[idx, :, seg] + jnp.sum(p, axis=0, keepdims=True)
    acc_ref[idx, :, seg] = (alpha * acc_ref[idx, :, seg]
                            + jnp.dot(vt, p.astype(BF16), preferred_element_type=F32))
    m_ref[idx, :, seg] = m_new


def _query_segments(t):
    n = max(t // QUERY_SEG, 1)
    return [slice(i * (t // n), (i + 1) * (t // n)) for i in range(n)]


def _fox_prompt_kernel(q_ref, k_ref, vt_ref, ck_ref, o_ref, m_ref, l_ref, acc_ref, *, t, group):
    qi = pl.program_id(2)
    _softmax_init(m_ref, l_ref, acc_ref)
    reps = t // HEAD_DIM

    def tile(kt, mask):
        rows = pl.ds(pl.multiple_of(kt * t, t), t)
        for h in range(group):
            cols = slice(h * HEAD_DIM, (h + 1) * HEAD_DIM)
            ck = ck_ref[h, rows, :]
            st = _nt_dot(k_ref[rows, cols], q_ref[:, cols]) - jnp.concatenate([ck] * reps, axis=1)
            if mask is not None:
                st = jnp.where(mask, st, NEG)
            _softmax_step_t(st, vt_ref[cols, rows], m_ref, l_ref, acc_ref, h)

    def body(kt, carry):
        tile(kt, None)
        return carry

    lax.fori_loop(0, qi, body, 0)
    key = lax.broadcasted_iota(jnp.int32, (t, t), 0)
    query = lax.broadcasted_iota(jnp.int32, (t, t), 1)
    tile(qi, key <= query)
    for h in range(group):
        o_ref[:, h * HEAD_DIM:(h + 1) * HEAD_DIM] = (acc_ref[h] / l_ref[h]).T.astype(BF16)


def fox_prompt(q, k, vt, ck, nb, seq, t, group):
    heads = q.shape[1] // HEAD_DIM
    group = min(group, heads)
    gw = group * HEAD_DIM
    nq = seq // t
    return pl.pallas_call(
        functools.partial(_fox_prompt_kernel, t=t, group=group),
        grid=(nb, heads // group, nq),
        in_specs=[pl.BlockSpec((t, gw), lambda b, h, i: (b * nq + i, h)),
                  pl.BlockSpec((seq, gw), lambda b, h, i: (b, h)),
                  pl.BlockSpec((gw, seq), lambda b, h, i: (h, b)),
                  pl.BlockSpec((None, group, seq, HEAD_DIM), lambda b, h, i: (b, h, 0, 0))],
        out_specs=pl.BlockSpec((t, gw), lambda b, h, i: (b * nq + i, h)),
        out_shape=jax.ShapeDtypeStruct((nb * seq, heads * HEAD_DIM), BF16),
        scratch_shapes=[pltpu.VMEM((group, 1, t), F32), pltpu.VMEM((group, 1, t), F32),
                        pltpu.VMEM((group, HEAD_DIM, t), F32)],
        compiler_params=_params("parallel", "parallel", "arbitrary"),
        name="fox_prompt",
    )(q, k, vt, ck)


def _diff_prompt_kernel(q_ref, k_ref, vt_ref, b0_ref, b1_ref, lam_ref, go_ref, o_ref,
                        m_ref, l_ref, acc_ref, *, t, group, out_scale):
    qi = pl.program_id(2)
    dv = 2 * HEAD_DIM
    _softmax_init(m_ref, l_ref, acc_ref)

    def tile(kt, bias_ref):
        rows = pl.ds(pl.multiple_of(kt * t, t), t)
        for h in range(group):
            vt = vt_ref[h * dv:(h + 1) * dv, rows]
            for c in range(2):
                cols = slice((2 * h + c) * HEAD_DIM, (2 * h + c + 1) * HEAD_DIM)
                for seg in _query_segments(t):
                    st = _nt_dot(k_ref[rows, cols], q_ref[seg, cols])
                    if bias_ref is not None:
                        st = st + bias_ref[h, :, seg]
                    _softmax_step_t(st, vt, m_ref, l_ref, acc_ref, 2 * h + c, seg)

    def body(kt, carry):
        tile(kt, None)
        return carry

    lax.fori_loop(0, jnp.maximum(qi - 1, 0), body, 0)

    @pl.when(qi >= 1)
    def _():
        tile(qi - 1, b1_ref)

    tile(qi, b0_ref)
    lam = lam_ref[...]
    for h in range(group):
        ot = acc_ref[2 * h] / l_ref[2 * h] - lam * (acc_ref[2 * h + 1] / l_ref[2 * h + 1])
        ms = jnp.mean(ot * ot, axis=0, keepdims=True)
        ot = ot * lax.rsqrt(ms + EPS) * (go_ref[...] * out_scale)
        o_ref[:, h * dv:(h + 1) * dv] = ot.T.astype(BF16)


def diff_prompt(q, k, vt, bias_diag, bias_prev, lam, g_o, out_scale, nb, seq, t, group):
    dv = 2 * HEAD_DIM
    heads = q.shape[1] // dv
    group = min(group, heads)
    gw = group * dv
    nq = seq // t
    bias = pl.BlockSpec((group, t, t), lambda b, h, i: (h, 0, 0))
    return pl.pallas_call(
        functools.partial(_diff_prompt_kernel, t=t, group=group, out_scale=out_scale),
        grid=(nb, heads // group, nq),
        in_specs=[pl.BlockSpec((t, gw), lambda b, h, i: (b * nq + i, h)),
                  pl.BlockSpec((seq, gw), lambda b, h, i: (b, h)),
                  pl.BlockSpec((gw, seq), lambda b, h, i: (h, b)),
                  bias, bias,
                  pl.BlockSpec((1, 1), lambda b, h, i: (0, 0)),
                  pl.BlockSpec((dv, 1), lambda b, h, i: (0, 0))],
        out_specs=pl.BlockSpec((t, gw), lambda b, h, i: (b * nq + i, h)),
        out_shape=jax.ShapeDtypeStruct((nb * seq, heads * dv), BF16),
        scratch_shapes=[pltpu.VMEM((2 * group, 1, t), F32), pltpu.VMEM((2 * group, 1, t), F32),
                        pltpu.VMEM((2 * group, dv, t), F32)],
        compiler_params=_params("parallel", "parallel", "arbitrary"),
        name="diff_prompt",
    )(q, k, vt, bias_diag, bias_prev, lam, g_o.reshape(dv, 1))


def _two_block_softmax(s_cache, s_new, v_cache, v_new):
    m = jnp.maximum(jnp.max(s_cache, axis=-1, keepdims=True), jnp.max(s_new, axis=-1, keepdims=True))
    p_cache = jnp.exp2(s_cache - m)
    p_new = jnp.exp2(s_new - m)
    l = jnp.sum(p_cache, axis=-1, keepdims=True) + jnp.sum(p_new, axis=-1, keepdims=True)
    o = (jnp.dot(p_cache.astype(BF16), v_cache, preferred_element_type=F32)
         + jnp.dot(p_new.astype(BF16), v_new, preferred_element_type=F32))
    return o / l


def _fox_sample_kernel(q_ref, kc_ref, vc_ref, ckc_ref, kn_ref, vn_ref, ckn_ref, o_ref, *, heads):
    tq = q_ref.shape[0]
    past = kc_ref.shape[0] // heads
    ri = lax.broadcasted_iota(jnp.int32, (tq, tq), 0)
    ci = lax.broadcasted_iota(jnp.int32, (tq, tq), 1)
    for h in range(heads):
        cols = slice(h * HEAD_DIM, (h + 1) * HEAD_DIM)
        head_rows = pl.ds(h, past, stride=heads)
        q = q_ref[:, cols]
        s_cache = _nt_dot(q, kc_ref[head_rows, :].astype(BF16)) - ckc_ref[h:h + 1, :]
        s_new = jnp.where(ci <= ri, _nt_dot(q, kn_ref[:, cols]) - ckn_ref[h:h + 1, :], NEG)
        o = _two_block_softmax(s_cache, s_new, vc_ref[head_rows, :].astype(BF16), vn_ref[:, cols])
        o_ref[:, cols] = o.astype(BF16)


def fox_sample(q, k, v, cache_k, cache_v, layer, ck_cache, ck_new, row0, nb, tq):
    w = q.shape[1]
    heads = w // HEAD_DIM
    past = cache_k.shape[2] // heads
    blk0 = row0 // tq
    new = pl.BlockSpec((tq, w), lambda b: (blk0 + b, 0))
    cache = pl.BlockSpec((None, None, past * heads, HEAD_DIM), lambda b: (layer, b, 0, 0))
    return pl.pallas_call(
        functools.partial(_fox_sample_kernel, heads=heads),
        grid=(nb,),
        in_specs=[new, cache, cache,
                  pl.BlockSpec((None, heads, past), lambda b: (b, 0, 0)),
                  new, new,
                  pl.BlockSpec((None, heads, tq), lambda b: (b, 0, 0))],
        out_specs=pl.BlockSpec((tq, w), lambda b: (b, 0)),
        out_shape=jax.ShapeDtypeStruct((nb * tq, w), BF16),
        compiler_params=_params("parallel"),
        name="fox_sample",
    )(q, cache_k, cache_v, ck_cache, k, v, ck_new)


def _diff_sample_kernel(q_ref, kc_ref, vc_ref, bc_ref, kn_ref, vn_ref, bn_ref, lam_ref, go_ref,
                        o_ref, *, heads, out_scale):
    dv = 2 * HEAD_DIM
    past = vc_ref.shape[0] // (2 * heads)
    lam = lam_ref[...]
    for h in range(heads):
        halves = [vc_ref[pl.ds(half * heads + h, past, stride=2 * heads), :] for half in range(2)]
        v_cache = jnp.concatenate(halves, axis=1).astype(BF16)
        v_new = vn_ref[:, h * dv:(h + 1) * dv]
        maps = []
        for c in range(2):
            cols = slice((2 * h + c) * HEAD_DIM, (2 * h + c + 1) * HEAD_DIM)
            map_rows = pl.ds(2 * h + c, past, stride=2 * heads)
            q = q_ref[:, cols]
            s_cache = _nt_dot(q, kc_ref[map_rows, :].astype(BF16)) + bc_ref[h]
            s_new = _nt_dot(q, kn_ref[:, cols]) + bn_ref[h]
            maps.append(_two_block_softmax(s_cache, s_new, v_cache, v_new))
        o = maps[0] - lam * maps[1]
        ms = jnp.mean(o * o, axis=-1, keepdims=True)
        o_ref[:, h * dv:(h + 1) * dv] = (o * lax.rsqrt(ms + EPS) * (go_ref[...] * out_scale)).astype(BF16)


def diff_sample(q, k, v, cache_k, cache_v, layer, bias_cache, bias_new, lam, g_o, out_scale,
                row0, nb, tq):
    w = q.shape[1]
    dv = 2 * HEAD_DIM
    heads = w // dv
    past = cache_v.shape[2] // (2 * heads)
    blk0 = row0 // tq
    new = pl.BlockSpec((tq, w), lambda b: (blk0 + b, 0))
    cache = pl.BlockSpec((None, None, past * 2 * heads, HEAD_DIM), lambda b: (layer, b, 0, 0))
    return pl.pallas_call(
        functools.partial(_diff_sample_kernel, heads=heads, out_scale=out_scale),
        grid=(nb,),
        in_specs=[new, cache, cache,
                  pl.BlockSpec((heads, tq, past), lambda b: (0, 0, 0)),
                  new, new,
                  pl.BlockSpec((heads, tq, tq), lambda b: (0, 0, 0)),
                  pl.BlockSpec((1, 1), lambda b: (0, 0)),
                  pl.BlockSpec((1, dv), lambda b: (0, 0))],
        out_specs=pl.BlockSpec((tq, w), lambda b: (b, 0)),
        out_shape=jax.ShapeDtypeStruct((nb * tq, w), BF16),
        compiler_params=_params("parallel"),
        name="diff_sample",
    )(q, cache_k, cache_v, bias_cache, k, v, bias_new, lam, g_o.reshape(1, dv))


def _t5_bucket(rel):
    half = NUM_BUCKETS // 2
    max_exact = half // 2
    ret = jnp.where(rel > 0, half, 0)
    n = jnp.abs(rel)
    nf = jnp.maximum(n, 1).astype(F32)
    large = max_exact + (jnp.log(nf / max_exact) / math.log(MAX_DISTANCE / max_exact)
                         * (half - max_exact)).astype(jnp.int32)
    large = jnp.minimum(large, half - 1)
    return ret + jnp.where(n < max_exact, n, large)


def _rel_bias(table, q_pos, k_pos):
    bucket = _t5_bucket(k_pos[None, :] - q_pos[:, None])[None]
    table = table.astype(F32)
    bias = sum(jnp.where(bucket == b, table[b][:, None, None], 0.0) for b in range(NUM_BUCKETS))
    mask = (k_pos[None, :] // CHUNK) <= (q_pos[:, None] // CHUNK)
    return bias, mask


def _forward(cfg, x_prompt, x_sample, c_prompt, c_sample, cache_diff_k, cache_diff_v, cache_fox_k,
             cache_fox_v, cache_fox_logf, state_hgrn, rel_bias_table, hgrn_lb_logits, w_ada, b_ada,
             g_ffn1, w_ffn1_up, w_ffn1_down, g_mix, w_in, b_fox_f, g_hg_o, g_diff_q, g_diff_k,
             diff_lambda, g_diff_o, g_fox_q, g_fox_k, w_branch, w_out, g_ffn2, w_ffn2_up,
             w_ffn2_down):
    d, w = cfg.d_model, cfg.width
    nb_p, seq, nb_s, tq_s, past = cfg.batch, cfg.seq, cfg.dec_batch, cfg.dec_seq, cfg.past_len
    ntp, nt = cfg.nt_prompt, cfg.nt
    tm, tmb, tn, t = cfg.tm, cfg.tm_big, cfg.tn, cfg.tq
    assert tq_s == SEG and seq % t == 0 and t >= MAX_DISTANCE and t % CHUNK == 0
    fox_h, diff_h, hg_h = cfg.fox_heads, cfg.diff_heads, cfg.hg_heads

    n_seq = nb_p + nb_s
    n_seq_pad = ((n_seq + 7) // 8) * 8
    c_all = jnp.concatenate([c_prompt, c_sample, jnp.zeros((n_seq_pad - n_seq, d), F32)], axis=0)
    mod = ada_mods(c_all, w_ada, b_ada, min(cfg.t_ada, N_ADA * d))
    mod = mod.reshape(cfg.depth, n_seq_pad, N_ADA, d)
    mod_p = jnp.broadcast_to(mod[:, :nb_p, None], (cfg.depth, nb_p, seq // SEG, N_ADA, d))
    mod = jnp.concatenate([mod_p.reshape(cfg.depth, -1, N_ADA, d), mod[:, nb_p:n_seq]], axis=1)
    mod = jnp.transpose(mod, (0, 2, 1, 3))[:, :, :, None, :]

    x3 = jnp.concatenate([x_prompt.reshape(-1, SEG, d), x_sample.reshape(-1, SEG, d)], axis=0)

    cs = jnp.cumsum(jax.nn.softmax(hgrn_lb_logits.astype(F32), axis=0), axis=0)
    lbs = cs - cs[0:1]

    far = rel_bias_table[NUM_BUCKETS // 2 - 1].astype(F32)[:, None, None]
    pos_t = jnp.arange(t, dtype=jnp.int32)
    bias_d, mask_d = _rel_bias(rel_bias_table, pos_t + t, pos_t + t)
    bias_diag = jnp.swapaxes(jnp.where(mask_d[None], LOG2E * (bias_d - far), NEG), 1, 2)
    bias_prev = jnp.swapaxes(LOG2E * (_rel_bias(rel_bias_table, pos_t + t, pos_t)[0] - far), 1, 2)
    q_pos_s = past + jnp.arange(tq_s, dtype=jnp.int32)
    bias_s, mask_s = _rel_bias(rel_bias_table, q_pos_s, jnp.arange(past + tq_s, dtype=jnp.int32))
    bias_s = jnp.where(mask_s[None], LOG2E * bias_s, NEG)
    bias_s_cache, bias_s_new = bias_s[:, :, :past], bias_s[:, :, past:]

    cache_dk = cache_diff_k.reshape(cfg.depth, nb_s, past * 2 * diff_h, HEAD_DIM)
    cache_dv = cache_diff_v.reshape(cfg.depth, nb_s, past, diff_h, 2, HEAD_DIM)
    cache_dv = jnp.transpose(cache_dv, (0, 1, 2, 4, 3, 5)).reshape(cfg.depth, nb_s, -1, HEAD_DIM)
    cache_fk = cache_fox_k.reshape(cfg.depth, nb_s, past * fox_h, HEAD_DIM)
    cache_fv = cache_fox_v.reshape(cfg.depth, nb_s, past * fox_h, HEAD_DIM)
    zero_state = jnp.zeros((1, nb_p, hg_h, HEAD_DIM, HEAD_DIM), F32)

    w_up1, w_up2 = w_ffn1_up.astype(BF16), w_ffn2_up.astype(BF16)
    wd1, wd2 = w_ffn1_down.astype(BF16), w_ffn2_down.astype(BF16)
    w_in_b = w_in.astype(BF16)
    w_gates = w_in_b[:, :, 10 * w + fox_h:]
    w_br = w_branch.astype(BF16)
    w_o = w_out.astype(BF16)

    diff_stacks = {"p": None, "s": None}
    fox_stacks = {"p": None, "s": None}
    states = {"p": [], "s": []}
    for li in range(cfg.depth):
        ml = mod[li]
        b_ff = jnp.pad(b_fox_f[li].astype(F32), (0, HEAD_DIM - fox_h)).reshape(1, HEAD_DIM)

        x3 = mm_residual(ffn_up(x3, g_ffn1[li], ml, 1, 0, w_up1, li, tmb, tn), wd1, li, x3, ml, 2,
                         0.5, tmb, tn)

        h = norm_mod(x3, g_mix[li], ml, 4, 3, tm)
        lb = lbs[li].reshape(1, w)
        q_hg, g_hg, k_hg, v_hg, og = proj_hgrn(h, w_in_b, li, jnp.log(lb), jnp.log1p(-lb), 1.0 - lb,
                                               tm)
        groups = {}
        for name, row0, nrows in (("p", 0, ntp), ("s", ntp, nt - ntp)):
            diff_qkv, diff_stacks[name] = proj_attn(
                h, w_in_b, w, 4, g_diff_q[li], g_diff_k[li], tm, row0, nrows, li, cfg.depth,
                diff_stacks[name], head_rows=True)
            fox_qkv, fox_stacks[name] = proj_attn(
                h, w_in_b, w, 7, g_fox_q[li], g_fox_k[li], tm, row0, nrows, li, cfg.depth,
                fox_stacks[name], forget_block=10 * w // HEAD_DIM, b_forget=b_ff)
            groups[name] = diff_qkv + fox_qkv
        dq_p, dk_p, dv_p, fq_p, fk_p, fv_p = groups["p"]
        dq_s, dk_s, dv_s, fq_s, fk_s, fv_s = groups["s"]
        gates = proj_gates(h, w_gates, li, cfg.tm_up, tn)

        o_hg_p, st_p = hgrn_mixer(q_hg, k_hg, v_hg, g_hg, og, zero_state, 0, g_hg_o[li], 0, nb_p, seq)
        o_hg_s, st_s = hgrn_mixer(q_hg, k_hg, v_hg, g_hg, og, state_hgrn.astype(F32), li,
                                  g_hg_o[li], ntp, nb_s, tq_s)

        lam_init = 0.8 - 0.6 * math.exp(-0.3 * li)
        dl = diff_lambda[li].astype(F32)
        lam = (jnp.exp(jnp.sum(dl[0] * dl[1])) - jnp.exp(jnp.sum(dl[2] * dl[3])) + lam_init).reshape(1, 1)
        o_df_p = diff_prompt(dq_p, dk_p, jnp.transpose(dv_p), bias_diag, bias_prev, lam,
                             g_diff_o[li], 1.0 - lam_init, nb_p, seq, t, cfg.diff_group)
        o_df_s = diff_sample(dq_s, dk_s, dv_s, cache_dk, cache_dv, li, bias_s_cache, bias_s_new, lam,
                             g_diff_o[li], 1.0 - lam_init, 0, nb_s, tq_s)

        logf_p = fox_stacks["p"][2][li, :, :fox_h].reshape(nb_p, seq, fox_h)
        logf_s = fox_stacks["s"][2][li, :, :fox_h].reshape(nb_s, tq_s, fox_h)
        ck_p = jnp.broadcast_to((LOG2E * cumsum_time(logf_p))[..., None],
                                (nb_p, fox_h, seq, HEAD_DIM))
        cum_s = LOG2E * cumsum_time(
            jnp.concatenate([cache_fox_logf[li].astype(F32), logf_s], axis=1))
        o_fx_p = fox_prompt(fq_p, fk_p, jnp.transpose(fv_p), ck_p, nb_p, seq, t, cfg.fox_group)
        o_fx_s = fox_sample(fq_s, fk_s, fv_s, cache_fk, cache_fv, li, cum_s[:, :, :past],
                            cum_s[:, :, past:], 0, nb_s, tq_s)

        merged = merge_branches((o_hg_p, o_df_p, o_fx_p), (o_hg_s, o_df_s, o_fx_s), gates, w_br, li,
                                tmb, tn)
        x3 = mm_residual(merged, w_o, li, x3, ml, 5, 1.0, tmb, tn)

        x3 = mm_residual(ffn_up(x3, g_ffn2[li], ml, 7, 6, w_up2, li, tmb, tn), wd2, li, x3, ml, 8,
                         0.5, tmb, tn)

        states["p"].append(st_p)
        states["s"].append(st_s)

    def group_outputs(name, nb, tlen):
        lead = (cfg.depth, nb, tlen)
        dk, dv = diff_stacks[name]
        fk, fv, lf = fox_stacks[name]
        dv = jnp.transpose(dv.reshape(lead + (2, diff_h, HEAD_DIM)), (0, 1, 2, 4, 3, 5))
        return (dk.reshape(lead + (diff_h, 2, HEAD_DIM)), dv.reshape(lead + (diff_h, 2 * HEAD_DIM)),
                fk.reshape(lead + (fox_h, HEAD_DIM)), fv.reshape(lead + (fox_h, HEAD_DIM)),
                lf[:, :, :fox_h].reshape(lead + (fox_h,)), jnp.stack(states[name]))

    y_prompt = x3[:ntp // SEG].reshape(nb_p, seq, d)
    y_sample = x3[ntp // SEG:].reshape(nb_s, tq_s, d)
    return (y_prompt, y_sample) + group_outputs("p", nb_p, seq) + group_outputs("s", nb_s, tq_s)


def kernel(x_prompt, x_sample, c_prompt, c_sample, cache_diff_k, cache_diff_v, cache_fox_k, cache_fox_v, cache_fox_logf, state_hgrn, rel_bias_table, hgrn_lb_logits, w_ada, b_ada, g_ffn1, w_ffn1_up, w_ffn1_down, g_mix, w_in, b_fox_f, g_hg_o, g_diff_q, g_diff_k, diff_lambda, g_diff_o, g_fox_q, g_fox_k, w_branch, w_out, g_ffn2, w_ffn2_up, w_ffn2_down):
    return _forward(FULL_CFG, x_prompt, x_sample, c_prompt, c_sample, cache_diff_k, cache_diff_v,
                    cache_fox_k, cache_fox_v, cache_fox_logf, state_hgrn, rel_bias_table,
                    hgrn_lb_logits, w_ada, b_ada, g_ffn1, w_ffn1_up, w_ffn1_down, g_mix, w_in,
                    b_fox_f, g_hg_o, g_diff_q, g_diff_k, diff_lambda, g_diff_o, g_fox_q, g_fox_k,
                    w_branch, w_out, g_ffn2, w_ffn2_up, w_ffn2_down)
```
